```python
import jax, jax.numpy as jnp
from jax import lax
import numpy as np

D_MODEL = 4096
BATCH = 2
SEQ = 4096
DEPTH = 2

POOL_WIDTH = D_MODEL // 4
POOL_WINDOWS = (2, 4, 8, 16)
POOL_GROUPS = len(POOL_WINDOWS)
POOL_GROUP_DIM = POOL_WIDTH // POOL_GROUPS

SB_HEAD_DIM = 128
SB_WIDTH = D_MODEL // 2
SB_HEADS = SB_WIDTH // SB_HEAD_DIM
SB_BLOCK = 128

RWKV_HEAD_DIM = 64
RWKV_WIDTH = D_MODEL // 4
RWKV_HEADS = RWKV_WIDTH // RWKV_HEAD_DIM
RWKV_DECAY_RANK = max(32, int(round(RWKV_WIDTH ** 0.5 * 1.8 / 32)) * 32)
RWKV_A_RANK = max(32, int(round(RWKV_WIDTH ** 0.5 * 1.8 / 32)) * 32)
RWKV_GATE_RANK = max(32, int(round(RWKV_WIDTH ** 0.8 * 0.6 / 32)) * 32)
RWKV_SHIFT_WIDTH = 3 * RWKV_WIDTH + RWKV_DECAY_RANK + RWKV_A_RANK + RWKV_GATE_RANK
RWKV_GN_EPS = 64e-5

N_BRANCH = 3
SPLIT_POOL = POOL_WIDTH
SPLIT_Q = SPLIT_POOL + SB_WIDTH
SPLIT_K = SPLIT_Q + SB_WIDTH
SPLIT_V = SPLIT_K + SB_WIDTH
SPLIT_RWKV = SPLIT_V + RWKV_SHIFT_WIDTH
N_IN = SPLIT_RWKV + N_BRANCH * D_MODEL

D_FF = ((8 * D_MODEL // 3 + 255) // 256) * 256
CONV_WIDTH = 3

DEEPNORM_ALPHA = (2 * DEPTH) ** 0.25
DEEPNORM_BETA = (8 * DEPTH) ** -0.25
LN_EPS = 1e-5

kernel_name = "hybrid_pool_stickbreak_rwkv7_deepnorm"


def layer_norm(x, g, b, eps=LN_EPS):
    xf = x.astype(jnp.float32)
    mu = jnp.mean(xf, axis=-1, keepdims=True)
    var = jnp.mean(jnp.square(xf - mu), axis=-1, keepdims=True)
    return ((xf - mu) * lax.rsqrt(var + eps) * g + b).astype(x.dtype)


def pool_mixer(u, pool_w, pool_scale):
    B, S, _ = u.shape
    ug = u.reshape(B, S, POOL_GROUPS, POOL_GROUP_DIM).astype(jnp.float32)
    csum = jnp.pad(jnp.cumsum(ug, axis=1), ((0, 0), (1, 0), (0, 0), (0, 0)))
    hi = jnp.arange(1, S + 1)
    diffs = []
    for g, win in enumerate(POOL_WINDOWS):
        lo = jnp.maximum(hi - win, 0)
        count = (hi - lo).astype(jnp.float32)[None, :, None]
        cg = csum[:, :, g]
        mean = (cg[:, hi] - cg[:, lo]) / count
        diffs.append(mean - ug[:, :, g])
    d = jnp.stack(diffs, axis=2).astype(u.dtype)
    y = jnp.einsum('bsgc,gcd->bsgd', d, pool_w)
    return y.reshape(B, S, POOL_WIDTH) * pool_scale


def stick_breaking_attention(q, k, v):
    B, S, H, Dh = q.shape
    scale = Dh ** -0.5
    qf = jnp.transpose(q, (0, 2, 1, 3)).astype(jnp.float32) * scale
    kf = jnp.transpose(k, (0, 2, 1, 3)).astype(jnp.float32)
    vf = jnp.transpose(v, (0, 2, 1, 3)).astype(jnp.float32)
    outs = []
    for i in range(S // SB_BLOCK):
        q0, q1 = i * SB_BLOCK, (i + 1) * SB_BLOCK
        z = jnp.einsum('bhqd,bhkd->bhqk', qf[:, :, q0:q1], kf[:, :, :q1])
        causal = jnp.arange(q1)[None, :] < jnp.arange(q0, q1)[:, None]
        log_keep = jnp.where(causal, jax.nn.log_sigmoid(-z), 0.0)
        later = lax.cumsum(log_keep, axis=3, reverse=True) - log_keep
        att = jnp.where(causal, jnp.exp(jax.nn.log_sigmoid(z) + later), 0.0)
        outs.append(jnp.einsum('bhqk,bhkd->bhqd', att, vf[:, :, :q1]))
    o = jnp.concatenate(outs, axis=2)
    return jnp.transpose(o, (0, 2, 1, 3)).astype(q.dtype)


def token_shift(u, mu):
    prev = jnp.pad(u, ((0, 0), (1, 0), (0, 0)))[:, :-1]
    return u + (prev - u) * mu


def rwkv7_mixer(u, w0, w2, a0, a2, g2, k_k, k_a, r_k, ln_g, ln_b):
    B, S, _ = u.shape
    W = RWKV_WIDTH
    r, k, v, dw, da, dg = jnp.split(
        u, [W, 2 * W, 3 * W, 3 * W + RWKV_DECAY_RANK, 3 * W + RWKV_DECAY_RANK + RWKV_A_RANK], axis=-1)
    w = -jax.nn.softplus(-(w0 + jnp.tanh(dw) @ w2)) - 0.5
    decay = jnp.exp(-jnp.exp(w.astype(jnp.float32)))
    a = jax.nn.sigmoid(a0 + da @ a2)
    g = jax.nn.sigmoid(dg) @ g2

    def heads(t):
        return t.reshape(B, S, RWKV_HEADS, RWKV_HEAD_DIM).astype(jnp.float32)

    kk = heads(k * k_k)
    kk = kk / jnp.maximum(jnp.sqrt(jnp.sum(kk * kk, axis=-1, keepdims=True)), 1e-12)
    k = k * (1 + (a - 1) * k_a)
    r_h, k_h, v_h, a_h, w_h = heads(r), heads(k), heads(v), heads(a), heads(decay)

    def tm(t):
        return jnp.transpose(t, (1, 0, 2, 3))

    def step(state, inp):
        r_t, w_t, k_t, v_t, kk_t, a_t = inp
        sa = jnp.einsum('bhij,bhj->bhi', state, -kk_t)
        state = (state * w_t[:, :, None, :]
                 + sa[..., :, None] * (kk_t * a_t)[..., None, :]
                 + v_t[..., :, None] * k_t[..., None, :])
        return state, jnp.einsum('bhij,bhj->bhi', state, r_t)

    s0 = jnp.zeros((B, RWKV_HEADS, RWKV_HEAD_DIM, RWKV_HEAD_DIM), jnp.float32)
    _, y = lax.scan(step, s0, (tm(r_h), tm(w_h), tm(k_h), tm(v_h), tm(kk), tm(a_h)))
    y = jnp.transpose(y, (1, 0, 2, 3))
    mu = jnp.mean(y, axis=-1, keepdims=True)
    var = jnp.mean(jnp.square(y - mu), axis=-1, keepdims=True)
    y = ((y - mu) * lax.rsqrt(var + RWKV_GN_EPS)).reshape(B, S, W) * ln_g + ln_b
    bonus = jnp.sum(r_h * k_h * r_k, axis=-1, keepdims=True) * v_h
    y = y + bonus.reshape(B, S, W)
    return (y * g).astype(u.dtype)


def causal_depthwise_conv(u, w, b):
    S = u.shape[1]
    up = jnp.pad(u, ((0, 0), (CONV_WIDTH - 1, 0), (0, 0)))
    out = up[:, 0:S] * w[0]
    for j in range(1, CONV_WIDTH):
        out = out + up[:, j:j + S] * w[j]
    return out + b


def setup_inputs(seed: int = 0) -> dict:
    key = jax.random.key(seed)
    ks = jax.random.split(key, 28)
    f32 = jnp.float32
    L = DEPTH

    def nrm(k, shape, scale):
        return jax.random.normal(k, shape, f32) * scale

    return {
        "x": nrm(ks[0], (BATCH, SEQ, D_MODEL), 1.0),
        "w_in": nrm(ks[1], (L, D_MODEL, N_IN), D_MODEL ** -0.5),
        "b_gate": nrm(ks[2], (L, N_BRANCH * D_MODEL), 0.02),
        "pool_w": nrm(ks[3], (L, POOL_GROUPS, POOL_GROUP_DIM, POOL_GROUP_DIM), POOL_GROUP_DIM ** -0.5),
        "pool_scale": 1.0 + nrm(ks[4], (L, POOL_WIDTH), 0.1),
        "rwkv_mu": jax.random.uniform(ks[5], (L, RWKV_SHIFT_WIDTH), f32),
        "rwkv_w0": jax.random.uniform(ks[6], (L, RWKV_WIDTH), f32, -6.0, 1.0),
        "rwkv_w2": nrm(ks[7], (L, RWKV_DECAY_RANK, RWKV_WIDTH), 0.1),
        "rwkv_a0": nrm(ks[8], (L, RWKV_WIDTH), 0.5),
        "rwkv_a2": nrm(ks[9], (L, RWKV_A_RANK, RWKV_WIDTH), 0.1),
        "rwkv_g2": nrm(ks[10], (L, RWKV_GATE_RANK, RWKV_WIDTH), RWKV_GATE_RANK ** -0.5),
        "rwkv_k_k": 0.85 + nrm(ks[11], (L, RWKV_WIDTH), 0.05),
        "rwkv_k_a": 1.0 + nrm(ks[12], (L, RWKV_WIDTH), 0.05),
        "rwkv_r_k": nrm(ks[13], (L, RWKV_HEADS, RWKV_HEAD_DIM), 0.1),
        "rwkv_ln_g": 1.0 + nrm(ks[14], (L, RWKV_WIDTH), 0.05),
        "rwkv_ln_b": nrm(ks[15], (L, RWKV_WIDTH), 0.02),
        "w_branch_pool": nrm(ks[16], (L, POOL_WIDTH, D_MODEL), POOL_WIDTH ** -0.5 * DEEPNORM_BETA),
        "w_branch_attn": nrm(ks[17], (L, SB_WIDTH, D_MODEL), SB_WIDTH ** -0.5 * DEEPNORM_BETA),
        "w_branch_rwkv": nrm(ks[18], (L, RWKV_WIDTH, D_MODEL), RWKV_WIDTH ** -0.5 * DEEPNORM_BETA),
        "w_out": nrm(ks[19], (L, D_MODEL, D_MODEL), D_MODEL ** -0.5 * DEEPNORM_BETA),
        "ln1_g": 1.0 + nrm(ks[20], (L, D_MODEL), 0.05),
        "ln1_b": nrm(ks[21], (L, D_MODEL), 0.02),
        "w_up": nrm(ks[22], (L, D_MODEL, 2 * D_FF), D_MODEL ** -0.5),
        "ffn_conv_w": nrm(ks[23], (L, CONV_WIDTH, D_FF), CONV_WIDTH ** -0.5),
        "ffn_conv_b": nrm(ks[24], (L, D_FF), 0.02),
        "w_down": nrm(ks[25], (L, D_FF, D_MODEL), D_FF ** -0.5 * DEEPNORM_BETA),
        "ln2_g": 1.0 + nrm(ks[26], (L, D_MODEL), 0.05),
        "ln2_b": nrm(ks[27], (L, D_MODEL), 0.02),
    }


def reference(x, w_in, b_gate, pool_w, pool_scale, rwkv_mu, rwkv_w0, rwkv_w2, rwkv_a0,
              rwkv_a2, rwkv_g2, rwkv_k_k, rwkv_k_a, rwkv_r_k, rwkv_ln_g, rwkv_ln_b,
              w_branch_pool, w_branch_attn, w_branch_rwkv, w_out, ln1_g, ln1_b,
              w_up, ffn_conv_w, ffn_conv_b, w_down, ln2_g, ln2_b):
    B, S, _ = x.shape
    for l in range(DEPTH):
        proj = x @ w_in[l]
        u_pool, q, k, v, u_rwkv, gate_logits = jnp.split(
            proj, [SPLIT_POOL, SPLIT_Q, SPLIT_K, SPLIT_V, SPLIT_RWKV], axis=-1)
        y_pool = pool_mixer(u_pool, pool_w[l], pool_scale[l])
        y_attn = stick_breaking_attention(
            q.reshape(B, S, SB_HEADS, SB_HEAD_DIM),
            k.reshape(B, S, SB_HEADS, SB_HEAD_DIM),
            v.reshape(B, S, SB_HEADS, SB_HEAD_DIM)).reshape(B, S, SB_WIDTH)
        y_rwkv = rwkv7_mixer(token_shift(u_rwkv, rwkv_mu[l]), rwkv_w0[l], rwkv_w2[l],
                             rwkv_a0[l], rwkv_a2[l], rwkv_g2[l], rwkv_k_k[l], rwkv_k_a[l],
                             rwkv_r_k[l], rwkv_ln_g[l], rwkv_ln_b[l])
        gates = jax.nn.sigmoid(gate_logits + b_gate[l]).reshape(B, S, N_BRANCH, D_MODEL)
        merged = (gates[:, :, 0] * (y_pool @ w_branch_pool[l])
                  + gates[:, :, 1] * (y_attn @ w_branch_attn[l])
                  + gates[:, :, 2] * (y_rwkv @ w_branch_rwkv[l]))
        x = layer_norm(DEEPNORM_ALPHA * x + merged @ w_out[l], ln1_g[l], ln1_b[l])
        act_in, lin = jnp.split(x @ w_up[l], [D_FF], axis=-1)
        act_in = causal_depthwise_conv(act_in, ffn_conv_w[l], ffn_conv_b[l])
        ffn = (jax.nn.gelu(act_in, approximate=False) * lin) @ w_down[l]
        x = layer_norm(DEEPNORM_ALPHA * x + ffn, ln2_g[l], ln2_b[l])
    return x
```

```python
import functools

import jax
import jax.numpy as jnp
from jax import lax
from jax.experimental import pallas as pl
from jax.experimental.pallas import tpu as pltpu

F32 = jnp.float32
BF16 = jnp.bfloat16

D_MODEL = 4096
DEPTH = 2
POOL_WIDTH = 1024
POOL_WINDOWS = (2, 4, 8, 16)
POOL_GROUP_DIM = 256
POOL_HALO = 16
SB_HEAD_DIM = 128
SB_WIDTH = 2048
SB_HEADS = 16
RWKV_WIDTH = 1024
RWKV_HEAD_DIM = 64
RWKV_DECAY_RANK = 64
RWKV_A_RANK = 64
RWKV_GATE_RANK = 160
RWKV_SHIFT_WIDTH = 3 * RWKV_WIDTH + RWKV_DECAY_RANK + RWKV_A_RANK + RWKV_GATE_RANK
RWKV_GN_EPS = 64e-5
N_MAIN = POOL_WIDTH + 3 * SB_WIDTH
RWKV_PAD = 3584
RWKV_LOW = 384
GATE_OFF = N_MAIN + RWKV_SHIFT_WIDTH
D_FF = 11008
DEEPNORM_ALPHA = (2 * DEPTH) ** 0.25
LN_EPS = 1e-5

LANES = 128
RWKV_CHUNK = 64
RWKV_PAIRS = RWKV_WIDTH // LANES
VMEM_BUDGET = 58 * 1024 * 1024


def _cparams(sem, vmem=None):
    return pltpu.CompilerParams(dimension_semantics=sem, vmem_limit_bytes=vmem)


def _dot(a, b, dims=None, prec=None):
    if dims is None:
        dims = (((a.ndim - 1,), (0,)), ((), ()))
    return lax.dot_general(a, b, dims, precision=prec, preferred_element_type=F32)


def _dot32(a, b, dims=None):
    return _dot(a, b, dims, prec=lax.Precision.HIGHEST)


NT = (((1,), (1,)), ((), ()))
TN = (((0,), (0,)), ((), ()))


def _mm_kernel(a_ref, w_ref, o_ref, wb_ref, *, kc, mc):
    k_dim = w_ref.shape[0]

    @pl.when(pl.program_id(1) == 0)
    def _():
        def cast(c, carry):
            r = pl.multiple_of(c * kc, kc)
            wb_ref[pl.ds(r, kc), :] = w_ref[pl.ds(r, kc), :].astype(BF16)
            return carry
        lax.fori_loop(0, k_dim // kc, cast, 0)

    def rows(c, carry):
        r = pl.multiple_of(c * mc, mc)
        o_ref[pl.ds(r, mc), :] = _dot(a_ref[pl.ds(r, mc), :], wb_ref[...]).astype(o_ref.dtype)
        return carry
    lax.fori_loop(0, a_ref.shape[0] // mc, rows, 0)


def _mm(a, w, *, n_cols, tn, tm, out_dtype, col_block_off=0):
    m_dim, k_dim = a.shape
    tm = min(tm, m_dim)
    mc = min(256, tm)
    return pl.pallas_call(
        functools.partial(_mm_kernel, kc=256, mc=mc),
        grid=(n_cols // tn, m_dim // tm),
        in_specs=[pl.BlockSpec((tm, k_dim), lambda j, i: (i, 0)),
                  pl.BlockSpec((k_dim, tn), lambda j, i: (0, j + col_block_off))],
        out_specs=pl.BlockSpec((tm, tn), lambda j, i: (i, j)),
        out_shape=jax.ShapeDtypeStruct((m_dim, n_cols), out_dtype),
        scratch_shapes=[pltpu.VMEM((k_dim, tn), BF16)],
        compiler_params=_cparams(("arbitrary", "arbitrary"), VMEM_BUDGET),
        name="mm",
    )(a, w)


def _ln_kernel(x_ref, y_ref, g_ref, b_ref, of_ref, ob_ref):
    z = DEEPNORM_ALPHA * x_ref[...] + y_ref[...]
    mu = jnp.mean(z, axis=-1, keepdims=True)
    zc = z - mu
    var = jnp.mean(zc * zc, axis=-1, keepdims=True)
    out = zc * lax.rsqrt(var + LN_EPS) * g_ref[...] + b_ref[...]
    of_ref[...] = out
    ob_ref[...] = out.astype(BF16)


def _ln(x, y, g, b):
    t_dim = x.shape[0]
    tm = min(128, t_dim)
    row = pl.BlockSpec((tm, D_MODEL), lambda i: (i, 0))
    vec = pl.BlockSpec((1, D_MODEL), lambda i: (0, 0))
    return pl.pallas_call(
        _ln_kernel, grid=(t_dim // tm,),
        in_specs=[row, row, vec, vec], out_specs=[row, row],
        out_shape=[jax.ShapeDtypeStruct((t_dim, D_MODEL), F32),
                   jax.ShapeDtypeStruct((t_dim, D_MODEL), BF16)],
        compiler_params=_cparams(("arbitrary",)),
        name="ln",
    )(x, y, g.reshape(1, D_MODEL), b.reshape(1, D_MODEL))


def _pool_kernel(u_ref, h_ref, pw_ref, ps_ref, o_ref, *, ts, tiles_per_seq):
    it = pl.program_id(0) % tiles_per_seq
    x = u_ref[...].astype(F32)
    halo = jnp.where(it == 0, 0.0, h_ref[...].astype(F32))
    xe = jnp.concatenate([halo, x], axis=0)
    t_seq = lax.broadcasted_iota(jnp.int32, (ts, 1), 0) + it * ts
    for g, win in enumerate(POOL_WINDOWS):
        cols = slice(g * POOL_GROUP_DIM, (g + 1) * POOL_GROUP_DIM)
        s = xe[:, cols]
        span = 1
        while span < win:
            s = s + pltpu.roll(s, span, axis=0)
            span *= 2
        cnt = jnp.minimum(t_seq + 1, win).astype(F32)
        d = s[POOL_HALO:, :] / cnt - x[:, cols]
        y = _dot(d.astype(BF16), pw_ref[g].astype(BF16))
        o_ref[:, cols] = (y * ps_ref[:, cols]).astype(o_ref.dtype)


def _pool(main, pool_w, pool_scale, seq):
    t_dim = main.shape[0]
    ts = min(512, seq)
    hb = ts // POOL_HALO
    return pl.pallas_call(
        functools.partial(_pool_kernel, ts=ts, tiles_per_seq=seq // ts),
        grid=(t_dim // ts,),
        in_specs=[pl.BlockSpec((ts, POOL_WIDTH), lambda i: (i, 0)),
                  pl.BlockSpec((POOL_HALO, POOL_WIDTH), lambda i: (jnp.maximum(i * hb - 1, 0), 0)),
                  pl.BlockSpec((4, POOL_GROUP_DIM, POOL_GROUP_DIM), lambda i: (0, 0, 0)),
                  pl.BlockSpec((1, POOL_WIDTH), lambda i: (0, 0))],
        out_specs=pl.BlockSpec((ts, POOL_WIDTH), lambda i: (i, 0)),
        out_shape=jax.ShapeDtypeStruct((t_dim, POOL_WIDTH), BF16),
        compiler_params=_cparams(("arbitrary",)),
        name="pool",
    )(main, main, pool_w, pool_scale.reshape(1, POOL_WIDTH))


def _attn_kernel(q_ref, k_ref, v_ref, r_ref, o_ref, *, tq, tk):
    iq = pl.program_id(2)
    q = q_ref[...]
    rmat = r_ref[...]
    scale = SB_HEAD_DIM ** -0.5
    ratio = tq // tk

    def block(j, carry, masked):
        acc, run = carry
        r0 = pl.multiple_of(j * tk, tk)
        kj = k_ref[pl.ds(r0, tk), :]
        vj = v_ref[pl.ds(r0, tk), :]
        z = _dot(q, kj, NT) * scale
        lk = -(jnp.maximum(z, 0.0) + jnp.log(1.0 + jnp.exp(-jnp.abs(z))))
        if masked:
            row = lax.broadcasted_iota(jnp.int32, (tq, tk), 0) + iq * tq
            col = lax.broadcasted_iota(jnp.int32, (tq, tk), 1) + j * tk
            causal = col < row
            lk = jnp.where(causal, lk, 0.0)
        hi = lk.astype(BF16)
        lo = (lk - hi.astype(F32)).astype(BF16)
        cs = _dot(hi, rmat) + _dot(lo, rmat)
        later = cs[:, :tk]
        total = cs[:, tk:]
        att = jnp.exp(z + lk + later + run)
        if masked:
            att = jnp.where(causal, att, 0.0)
        acc = acc + _dot(att.astype(BF16), vj)
        return acc, run + total

    carry = (jnp.zeros((tq, SB_HEAD_DIM), F32), jnp.zeros((tq, tk), F32))
    for d in range(ratio):
        carry = block(iq * ratio + (ratio - 1 - d), carry, True)
    n_full = iq * ratio
    carry = lax.fori_loop(0, n_full, lambda n, c: block(n_full - 1 - n, c, False), carry)
    o_ref[...] = carry[0].astype(o_ref.dtype)


def _attention(main, batch, seq):
    t_dim = main.shape[0]
    tq = min(256, seq)
    tk = LANES
    nq = seq // tq
    q_blk0 = POOL_WIDTH // LANES
    k_blk0 = q_blk0 + SB_HEADS
    v_blk0 = k_blk0 + SB_HEADS
    row = lax.broadcasted_iota(jnp.int32, (tk, tk), 0)
    col = lax.broadcasted_iota(jnp.int32, (tk, tk), 1)
    rmat = jnp.concatenate([(row > col).astype(BF16), jnp.ones((tk, tk), BF16)], axis=1)
    return pl.pallas_call(
        functools.partial(_attn_kernel, tq=tq, tk=tk),
        grid=(batch, SB_HEADS, nq),
        in_specs=[pl.BlockSpec((tq, SB_HEAD_DIM), lambda b, h, i: (b * nq + i, q_blk0 + h)),
                  pl.BlockSpec((seq, SB_HEAD_DIM), lambda b, h, i: (b, k_blk0 + h)),
                  pl.BlockSpec((seq, SB_HEAD_DIM), lambda b, h, i: (b, v_blk0 + h)),
                  pl.BlockSpec((tk, 2 * tk), lambda b, h, i: (0, 0))],
        out_specs=pl.BlockSpec((tq, SB_HEAD_DIM), lambda b, h, i: (b * nq + i, h)),
        out_shape=jax.ShapeDtypeStruct((t_dim, SB_WIDTH), BF16),
        compiler_params=_cparams(("arbitrary", "arbitrary", "arbitrary")),
        name="attn",
    )(main, main, main, rmat)


def _head_sum(x, bd):
    return jnp.concatenate(
        [_dot32(x[:, c * LANES:(c + 1) * LANES], bd) for c in range(x.shape[1] // LANES)], axis=1)


def _rwkv_prep_kernel(u_ref, h_ref, mu_ref, w0_ref, a0_ref, kk_ref, ka_ref, wc_ref, bd_ref,
                      r_o, k_o, v_o, lw_o, an_o, bn_o, g_o, *, ts, tiles_per_seq):
    it = pl.program_id(0) % tiles_per_seq
    u = u_ref[...]
    last = jnp.where(it == 0, 0.0, h_ref[7:8, :])
    row = lax.broadcasted_iota(jnp.int32, (ts, 1), 0)
    prev = jnp.where(row == 0, last, pltpu.roll(u, 1, axis=0))
    x = u + (prev - u) * mu_ref[...]
    w_dim = RWKV_WIDTH
    r, k, v = x[:, :w_dim], x[:, w_dim:2 * w_dim], x[:, 2 * w_dim:3 * w_dim]
    low = x[:, 3 * w_dim:3 * w_dim + RWKV_LOW]
    lane = lax.broadcasted_iota(jnp.int32, (1, RWKV_LOW), 1)
    low = jnp.where(lane < RWKV_DECAY_RANK, jnp.tanh(low),
                    jnp.where(lane < RWKV_DECAY_RANK + RWKV_A_RANK, low, jax.nn.sigmoid(low)))
    mm = _dot32(low, wc_ref[...])
    w_lin = w0_ref[...] + mm[:, :w_dim]
    soft = jnp.maximum(-w_lin, 0.0) + jnp.log(1.0 + jnp.exp(-jnp.abs(w_lin)))
    log_decay = -jnp.exp(-soft - 0.5)
    a = jax.nn.sigmoid(a0_ref[...] + mm[:, w_dim:2 * w_dim])
    g = mm[:, 2 * w_dim:]
    kk = k * kk_ref[...]
    norm = jnp.sqrt(_head_sum(kk * kk, bd_ref[...]))
    kk = kk / jnp.maximum(norm, 1e-12)
    r_o[...] = r
    k_o[...] = k * (1.0 + (a - 1.0) * ka_ref[...])
    v_o[...] = v
    lw_o[...] = log_decay
    an_o[...] = -kk
    bn_o[...] = kk * a
    g_o[...] = g


def _rwkv_prep(urw, seq, mu, w0, a0, k_k, k_a, wc, bd):
    t_dim = urw.shape[0]
    ts = min(256, seq)
    hb = ts // 8
    wide = pl.BlockSpec((ts, RWKV_PAD), lambda i: (i, 0))
    halo = pl.BlockSpec((8, RWKV_PAD), lambda i: (jnp.maximum(i * hb - 1, 0), 0))
    vec = pl.BlockSpec((1, RWKV_WIDTH), lambda i: (0, 0))
    out = pl.BlockSpec((ts, RWKV_WIDTH), lambda i: (i, 0))
    return pl.pallas_call(
        functools.partial(_rwkv_prep_kernel, ts=ts, tiles_per_seq=seq // ts),
        grid=(t_dim // ts,),
        in_specs=[wide, halo, pl.BlockSpec((1, RWKV_PAD), lambda i: (0, 0)), vec, vec, vec, vec,
                  pl.BlockSpec((RWKV_LOW, 3 * RWKV_WIDTH), lambda i: (0, 0)),
                  pl.BlockSpec((LANES, LANES), lambda i: (0, 0))],
        out_specs=[out] * 7,
        out_shape=[jax.ShapeDtypeStruct((t_dim, RWKV_WIDTH), F32)] * 7,
        compiler_params=_cparams(("arbitrary",), VMEM_BUDGET),
        name="rwkv_prep",
    )(urw, urw, mu, w0, a0, k_k, k_a, wc, bd)


def _rwkv_chunk_kernel(r_ref, k_ref, v_ref, lw_ref, an_ref, bn_ref,
                       m_o, n_o, q_o, y_o, *, pairs):
    c_len = RWKV_CHUNK
    n2 = 2 * c_len
    ri = lax.broadcasted_iota(jnp.int32, (n2, n2), 0)
    ci = lax.broadcasted_iota(jnp.int32, (n2, n2), 1)
    strict = (ri % c_len) > (ci % c_len)
    incl = (ri % c_len) >= (ci % c_len)
    eye = (ri == ci).astype(F32)
    tri = (lax.broadcasted_iota(jnp.int32, (c_len, c_len), 0)
           >= lax.broadcasted_iota(jnp.int32, (c_len, c_len), 1)).astype(F32)
    lane = lax.broadcasted_iota(jnp.int32, (n2, LANES), 1)
    rowh = lax.broadcasted_iota(jnp.int32, (n2, LANES), 0)
    own = (lane < RWKV_HEAD_DIM) == (rowh < c_len)

    def stack(x):
        return jnp.where(own, jnp.concatenate([x, x], axis=0), 0.0)

    for p in range(pairs):
        cols = slice(p * LANES, (p + 1) * LANES)
        lw = lw_ref[:, cols]
        cum = _dot32(tri, lw)
        tot = cum[c_len - 1:c_len, :]
        a_t = stack(an_ref[:, cols] * jnp.exp(cum - lw))
        r_t = stack(r_ref[:, cols] * jnp.exp(cum))
        b_t = stack(bn_ref[:, cols] * jnp.exp(-cum))
        k_t = stack(k_ref[:, cols] * jnp.exp(-cum))
        b_e = stack(bn_ref[:, cols] * jnp.exp(tot - cum))
        k_e = stack(k_ref[:, cols] * jnp.exp(tot - cum))
        v_s = stack(v_ref[:, cols])
        a_ab = jnp.where(strict, _dot32(a_t, b_t, NT), 0.0)
        a_ak = jnp.where(strict, _dot32(a_t, k_t, NT), 0.0)
        r_ab = jnp.where(incl, _dot32(r_t, b_t, NT), 0.0)
        r_ak = jnp.where(incl, _dot32(r_t, k_t, NT), 0.0)
        inv = eye + a_ab
        pw = a_ab
        span = 2
        while span < c_len:
            pw = _dot32(pw, pw)
            inv = inv + _dot32(inv, pw)
            span *= 2
        w_s = _dot32(inv, a_t)
        x_s = _dot32(inv, _dot32(a_ak, v_s))
        q_s = r_t + _dot32(r_ab, w_s)
        y_s = _dot32(r_ab, x_s) + _dot32(r_ak, v_s)
        decay_end = jnp.exp(tot)
        m_o[0, p] = eye * decay_end + _dot32(b_e, w_s, TN)
        n_o[0, p] = _dot32(b_e, x_s, TN) + _dot32(k_e, v_s, TN)
        q_o[:, cols] = q_s[:c_len] + q_s[c_len:]
        y_o[:, cols] = y_s[:c_len] + y_s[c_len:]


def _rwkv_chunks(r, k, v, lw, an, bn):
    t_dim = r.shape[0]
    pairs = 2
    n_chunks = t_dim // RWKV_CHUNK
    tile = pl.BlockSpec((RWKV_CHUNK, pairs * LANES), lambda c, p: (c, p))
    op = pl.BlockSpec((1, pairs, LANES, LANES), lambda c, p: (c, p, 0, 0))
    op_shape = jax.ShapeDtypeStruct((n_chunks, RWKV_PAIRS, LANES, LANES), F32)
    row_shape = jax.ShapeDtypeStruct((t_dim, RWKV_WIDTH), F32)
    return pl.pallas_call(
        functools.partial(_rwkv_chunk_kernel, pairs=pairs),
        grid=(n_chunks, RWKV_PAIRS // pairs),
        in_specs=[tile] * 6,
        out_specs=[op, op, tile, tile],
        out_shape=[op_shape, op_shape, row_shape, row_shape],
        compiler_params=_cparams(("arbitrary", "arbitrary")),
        name="rwkv_chunk",
    )(r, k, v, lw, an, bn)


def _rwkv_sweep_kernel(m_ref, n_ref, q_ref, y0_ref, y_o, z_ref, *, pairs, chunks):
    @pl.when(pl.program_id(2) == 0)
    def _():
        z_ref[...] = jnp.zeros_like(z_ref)

    for p in range(pairs):
        cols = slice(p * LANES, (p + 1) * LANES)
        z = z_ref[p]
        for c in range(chunks):
            rows = slice(c * RWKV_CHUNK, (c + 1) * RWKV_CHUNK)
            y_o[rows, cols] = _dot32(q_ref[rows, cols], z) + y0_ref[rows, cols]
            z = _dot32(m_ref[c, p], z) + n_ref[c, p]
        z_ref[p] = z


def _rwkv_sweep(m_op, n_op, q_op, y0, batch, seq):
    t_dim = q_op.shape[0]
    pairs = 4
    chunks = min(4, seq // RWKV_CHUNK)
    steps = seq // (RWKV_CHUNK * chunks)
    tile = pl.BlockSpec((RWKV_CHUNK * chunks, pairs * LANES), lambda b, p, s: (b * steps + s, p))
    op = pl.BlockSpec((chunks, pairs, LANES, LANES), lambda b, p, s: (b * steps + s, p, 0, 0))
    return pl.pallas_call(
        functools.partial(_rwkv_sweep_kernel, pairs=pairs, chunks=chunks),
        grid=(batch, RWKV_PAIRS // pairs, steps),
        in_specs=[op, op, tile, tile],
        out_specs=tile,
        out_shape=jax.ShapeDtypeStruct((t_dim, RWKV_WIDTH), F32),
        scratch_shapes=[pltpu.VMEM((pairs, LANES, LANES), F32)],
        compiler_params=_cparams(("arbitrary", "arbitrary", "arbitrary")),
        name="rwkv_sweep",
    )(m_op, n_op, q_op, y0)


def _rwkv_out_kernel(y_ref, r_ref, k_ref, v_ref, g_ref, rk_ref, lg_ref, lb_ref, bd_ref, o_ref):
    bd = bd_ref[...]
    y = y_ref[...]
    mu = _head_sum(y, bd) * (1.0 / RWKV_HEAD_DIM)
    yc = y - mu
    var = _head_sum(yc * yc, bd) * (1.0 / RWKV_HEAD_DIM)
    yn = yc * lax.rsqrt(var + RWKV_GN_EPS) * lg_ref[...] + lb_ref[...]
    bonus = _head_sum(r_ref[...] * k_ref[...] * rk_ref[...], bd) * v_ref[...]
    o_ref[...] = ((yn + bonus) * g_ref[...]).astype(o_ref.dtype)


def _rwkv_out(y, r, k, v, g, r_k, ln_g, ln_b, bd):
    t_dim = y.shape[0]
    ts = min(256, t_dim)
    tile = pl.BlockSpec((ts, RWKV_WIDTH), lambda i: (i, 0))
    vec = pl.BlockSpec((1, RWKV_WIDTH), lambda i: (0, 0))
    return pl.pallas_call(
        _rwkv_out_kernel, grid=(t_dim // ts,),
        in_specs=[tile] * 5 + [vec] * 3 + [pl.BlockSpec((LANES, LANES), lambda i: (0, 0))],
        out_specs=tile,
        out_shape=jax.ShapeDtypeStruct((t_dim, RWKV_WIDTH), BF16),
        compiler_params=_cparams(("arbitrary",)),
        name="rwkv_out",
    )(y, r, k, v, g, r_k, ln_g, ln_b, bd)


def _merge_kernel(yp_ref, ya_ref, yr_ref, wp_ref, wa_ref, wr_ref, g0_ref, g1_ref, g2_ref,
                  b0_ref, b1_ref, b2_ref, o_ref, wpb, wab, wrb):
    @pl.when(pl.program_id(1) == 0)
    def _():
        wpb[...] = wp_ref[...].astype(BF16)
        wab[...] = wa_ref[...].astype(BF16)
        wrb[...] = wr_ref[...].astype(BF16)

    m = jax.nn.sigmoid(g0_ref[...] + b0_ref[...]) * _dot(yp_ref[...], wpb[...])
    m = m + jax.nn.sigmoid(g1_ref[...] + b1_ref[...]) * _dot(ya_ref[...], wab[...])
    m = m + jax.nn.sigmoid(g2_ref[...] + b2_ref[...]) * _dot(yr_ref[...], wrb[...])
    o_ref[...] = m.astype(o_ref.dtype)


def _merge(y_pool, y_attn, y_rwkv, w_pool, w_attn, w_rwkv, gate_logits, b_gate):
    t_dim = y_pool.shape[0]
    tm = min(512, t_dim)
    tn = 512
    nb = D_MODEL // tn
    act = lambda width: pl.BlockSpec((tm, width), lambda j, i: (i, 0))
    wgt = lambda width: pl.BlockSpec((width, tn), lambda j, i: (0, j))
    gate = lambda br: pl.BlockSpec((tm, tn), lambda j, i: (i, br * nb + j))
    bias = lambda br: pl.BlockSpec((1, tn), lambda j, i: (0, br * nb + j))
    return pl.pallas_call(
        _merge_kernel, grid=(nb, t_dim // tm),
        in_specs=[act(POOL_WIDTH), act(SB_WIDTH), act(RWKV_WIDTH),
                  wgt(POOL_WIDTH), wgt(SB_WIDTH), wgt(RWKV_WIDTH),
                  gate(0), gate(1), gate(2), bias(0), bias(1), bias(2)],
        out_specs=pl.BlockSpec((tm, tn), lambda j, i: (i, j)),
        out_shape=jax.ShapeDtypeStruct((t_dim, D_MODEL), BF16),
        scratch_shapes=[pltpu.VMEM((POOL_WIDTH, tn), BF16), pltpu.VMEM((SB_WIDTH, tn), BF16),
                        pltpu.VMEM((RWKV_WIDTH, tn), BF16)],
        compiler_params=_cparams(("arbitrary", "arbitrary"), VMEM_BUDGET),
        name="merge",
    )(y_pool, y_attn, y_rwkv, w_pool, w_attn, w_rwkv, gate_logits, gate_logits, gate_logits,
      b_gate, b_gate, b_gate)


def _ffn_act_kernel(a_ref, h_ref, l_ref, cw_ref, cb_ref, o_ref, *, ts, tiles_per_seq):
    it = pl.program_id(0) % tiles_per_seq
    x = a_ref[...]
    halo = jnp.where(it == 0, 0.0, h_ref[...])
    row = lax.broadcasted_iota(jnp.int32, (ts, 1), 0)
    p1 = jnp.where(row == 0, halo[7:8, :], pltpu.roll(x, 1, axis=0))
    p2 = jnp.where(row == 0, halo[6:7, :], jnp.where(row == 1, halo[7:8, :], pltpu.roll(x, 2, axis=0)))
    c = p2 * cw_ref[0:1, :] + p1 * cw_ref[1:2, :] + x * cw_ref[2:3, :] + cb_ref[...]
    gelu = 0.5 * c * (1.0 + lax.erf(c * (2.0 ** -0.5)))
    o_ref[...] = (gelu * l_ref[...]).astype(o_ref.dtype)


def _ffn_act(up, conv_w, conv_b, seq):
    t_dim = up.shape[0]
    ts = min(256, seq)
    tn = 256
    nb = D_FF // tn
    hb = ts // 8
    return pl.pallas_call(
        functools.partial(_ffn_act_kernel, ts=ts, tiles_per_seq=seq // ts),
        grid=(t_dim // ts, nb),
        in_specs=[pl.BlockSpec((ts, tn), lambda i, j: (i, j)),
                  pl.BlockSpec((8, tn), lambda i, j: (jnp.maximum(i * hb - 1, 0), j)),
                  pl.BlockSpec((ts, tn), lambda i, j: (i, nb + j)),
                  pl.BlockSpec((3, tn), lambda i, j: (0, j)),
                  pl.BlockSpec((1, tn), lambda i, j: (0, j))],
        out_specs=pl.BlockSpec((ts, tn), lambda i, j: (i, j)),
        out_shape=jax.ShapeDtypeStruct((t_dim, D_FF), BF16),
        compiler_params=_cparams(("arbitrary", "arbitrary")),
        name="ffn_act",
    )(up, up, up, conv_w, conv_b.reshape(1, D_FF))


def kernel(x, w_in, b_gate, pool_w, pool_scale, rwkv_mu, rwkv_w0, rwkv_w2, rwkv_a0, rwkv_a2, rwkv_g2, rwkv_k_k, rwkv_k_a, rwkv_r_k, rwkv_ln_g, rwkv_ln_b, w_branch_pool, w_branch_attn, w_branch_rwkv, w_out, ln1_g, ln1_b, w_up, ffn_conv_w, ffn_conv_b, w_down, ln2_g, ln2_b):
    batch, seq, _ = x.shape
    t_dim = batch * seq
    xf = x.reshape(t_dim, D_MODEL)
    xb = xf.astype(BF16)
    half = lax.broadcasted_iota(jnp.int32, (LANES, LANES), 0) // RWKV_HEAD_DIM
    bd = (half == half.T).astype(F32)
    vec = lambda p: p.reshape(1, -1)
    for l in range(DEPTH):
        w = w_in[l]
        main = _mm(xb, w, n_cols=N_MAIN, tn=512, tm=1024, out_dtype=BF16)
        w_rw = jnp.pad(w[:, N_MAIN:GATE_OFF], ((0, 0), (0, RWKV_PAD - RWKV_SHIFT_WIDTH)))
        urw = _mm(xb, w_rw, n_cols=RWKV_PAD, tn=512, tm=1024, out_dtype=F32)
        gate_logits = _mm(xb, w[:, GATE_OFF:], n_cols=3 * D_MODEL, tn=512, tm=1024, out_dtype=F32)

        y_pool = _pool(main, pool_w[l], pool_scale[l], seq)
        y_attn = _attention(main, batch, seq)

        wc = jnp.zeros((RWKV_LOW, 3 * RWKV_WIDTH), F32)
        wc = wc.at[:RWKV_DECAY_RANK, :RWKV_WIDTH].set(rwkv_w2[l])
        wc = wc.at[RWKV_DECAY_RANK:RWKV_DECAY_RANK + RWKV_A_RANK, RWKV_WIDTH:2 * RWKV_WIDTH].set(rwkv_a2[l])
        lo = RWKV_DECAY_RANK + RWKV_A_RANK
        wc = wc.at[lo:lo + RWKV_GATE_RANK, 2 * RWKV_WIDTH:].set(rwkv_g2[l])
        mu = jnp.pad(rwkv_mu[l], (0, RWKV_PAD - RWKV_SHIFT_WIDTH)).reshape(1, RWKV_PAD)
        r, k, v, lw, an, bn, g = _rwkv_prep(urw, seq, mu, vec(rwkv_w0[l]), vec(rwkv_a0[l]),
                                            vec(rwkv_k_k[l]), vec(rwkv_k_a[l]), wc, bd)
        m_op, n_op, q_op, y0 = _rwkv_chunks(r, k, v, lw, an, bn)
        y_scan = _rwkv_sweep(m_op, n_op, q_op, y0, batch, seq)
        y_rwkv = _rwkv_out(y_scan, r, k, v, g, vec(rwkv_r_k[l]), vec(rwkv_ln_g[l]),
                           vec(rwkv_ln_b[l]), bd)

        merged = _merge(y_pool, y_attn, y_rwkv, w_branch_pool[l], w_branch_attn[l],
                        w_branch_rwkv[l], gate_logits, vec(b_gate[l]))
        sub = _mm(merged, w_out[l], n_cols=D_MODEL, tn=512, tm=1024, out_dtype=F32)
        xf, xb = _ln(xf, sub, ln1_g[l], ln1_b[l])

        up = _mm(xb, w_up[l], n_cols=2 * D_FF, tn=512, tm=1024, out_dtype=F32)
        h = _ffn_act(up, ffn_conv_w[l], ffn_conv_b[l], seq)
        ffn = _mm(h, w_down[l], n_cols=D_MODEL, tn=256, tm=512, out_dtype=F32)
        xf, xb = _ln(xf, ffn, ln2_g[l], ln2_b[l])
    return xf.reshape(batch, seq, D_MODEL)
```

```python
import functools
import math

import jax
import jax.numpy as jnp
from jax import lax
from jax.experimental import pallas as pl
from jax.experimental.pallas import tpu as pltpu

F32 = jnp.float32
BF16 = jnp.bfloat16

D_MODEL = 4096
DEPTH = 2
POOL_WIDTH = 1024
POOL_WINDOWS = (2, 4, 8, 16)
POOL_GROUP_DIM = 256
POOL_HALO = 16
SB_HEAD_DIM = 128
SB_WIDTH = 2048
SB_HEADS = 16
RWKV_WIDTH = 1024
RWKV_HEAD_DIM = 64
RWKV_DECAY_RANK = 64
RWKV_A_RANK = 64
RWKV_GATE_RANK = 160
RWKV_SHIFT_WIDTH = 3 * RWKV_WIDTH + RWKV_DECAY_RANK + RWKV_A_RANK + RWKV_GATE_RANK
RWKV_GN_EPS = 64e-5
N_MAIN = POOL_WIDTH + 3 * SB_WIDTH
RWKV_PAD = 3584
RWKV_LOW = 384
GATE_OFF = N_MAIN + RWKV_SHIFT_WIDTH
D_FF = 11008
DEEPNORM_ALPHA = (2 * DEPTH) ** 0.25
LN_EPS = 1e-5

LANES = 128
RWKV_CHUNK = 64
RWKV_PAIRS = RWKV_WIDTH // LANES
VMEM_BUDGET = 58 * 1024 * 1024
CAST_ROWS = 256
DOT_ROWS = 256


def _cparams(sem, vmem=None):
    return pltpu.CompilerParams(dimension_semantics=sem, vmem_limit_bytes=vmem)


def _dot(a, b, dims=None, prec=None):
    if dims is None:
        dims = (((a.ndim - 1,), (0,)), ((), ()))
    return lax.dot_general(a, b, dims, precision=prec, preferred_element_type=F32)


def _dot32(a, b, dims=None):
    return _dot(a, b, dims, prec=lax.Precision.HIGHEST)


def _split(a):
    hi = a.astype(BF16)
    return hi, (a - hi.astype(F32)).astype(BF16)


def _dot3s(a, b_split):
    ah, al = _split(a)
    bh, bl = b_split
    return _dot(ah, bh) + _dot(ah, bl) + _dot(al, bh)


NT = (((1,), (1,)), ((), ()))
TN = (((0,), (0,)), ((), ()))


def _cast_panel(w_refs, wb_ref, shift, width):
    k_dim = wb_ref.shape[0]

    def cast(c, carry):
        r = pl.multiple_of(c * CAST_ROWS, CAST_ROWS)
        parts = [w[pl.ds(r, CAST_ROWS), :] for w in w_refs]
        w = parts[0] if len(parts) == 1 else jnp.concatenate(parts, axis=1)
        wb_ref[pl.ds(r, CAST_ROWS), :] = w[:, shift:shift + width].astype(BF16)
        return carry
    lax.fori_loop(0, k_dim // CAST_ROWS, cast, 0)


def _mm_kernel(*refs, n_w, shift, scaled):
    a_ref, w_refs = refs[0], refs[1:1 + n_w]
    s_ref = refs[1 + n_w] if scaled else None
    o_ref, wb_ref = refs[-2], refs[-1]
    tm, tn = o_ref.shape

    @pl.when(pl.program_id(1) == 0)
    def _():
        _cast_panel(w_refs, wb_ref, shift, tn)

    for c in range(tm // DOT_ROWS):
        rows = slice(c * DOT_ROWS, (c + 1) * DOT_ROWS)
        res = _dot(a_ref[rows, :], wb_ref[...])
        if scaled:
            res = res * s_ref[...]
        o_ref[rows, :] = res.astype(o_ref.dtype)


def _mm(a, w, layer, *, col0, n_cols, out_dtype, tm=1024, tn=512, col_scale=None):
    m_dim, k_dim = a.shape
    tm = min(tm, m_dim)
    shift = col0 % LANES
    base = col0 - shift
    if shift == 0:
        assert base % tn == 0
        pieces = [(tn, base // tn)]
    else:
        half = tn // 2
        assert base % half == 0
        pieces = [(half, base // half), (half, base // half + 1), (LANES, (base + tn) // LANES)]
    w_specs = [pl.BlockSpec((None, k_dim, width), functools.partial(
        lambda j, i, width, first: (layer, 0, first + j * (tn // width)), width=width, first=first))
        for width, first in pieces]
    operands = [a] + [w] * len(pieces)
    in_specs = [pl.BlockSpec((tm, k_dim), lambda j, i: (i, 0))] + w_specs
    if col_scale is not None:
        operands.append(col_scale)
        in_specs.append(pl.BlockSpec((1, tn), lambda j, i: (0, j)))
    return pl.pallas_call(
        functools.partial(_mm_kernel, n_w=len(pieces), shift=shift, scaled=col_scale is not None),
        grid=(n_cols // tn, m_dim // tm),
        in_specs=in_specs,
        out_specs=pl.BlockSpec((tm, tn), lambda j, i: (i, j)),
        out_shape=jax.ShapeDtypeStruct((m_dim, n_cols), out_dtype),
        scratch_shapes=[pltpu.VMEM((k_dim, tn), BF16)],
        compiler_params=_cparams(("arbitrary", "arbitrary"), VMEM_BUDGET),
        name="mm",
    )(*operands)


def _ln_kernel(x_ref, y_ref, g_ref, b_ref, of_ref, ob_ref):
    z = DEEPNORM_ALPHA * x_ref[...] + y_ref[...]
    mu = jnp.mean(z, axis=-1, keepdims=True)
    zc = z - mu
    var = jnp.mean(zc * zc, axis=-1, keepdims=True)
    out = zc * lax.rsqrt(var + LN_EPS) * g_ref[...] + b_ref[...]
    of_ref[...] = out
    ob_ref[...] = out.astype(BF16)


def _ln(x, y, g, b):
    t_dim = x.shape[0]
    tm = min(128, t_dim)
    row = pl.BlockSpec((tm, D_MODEL), lambda i: (i, 0))
    vec = pl.BlockSpec((1, D_MODEL), lambda i: (0, 0))
    return pl.pallas_call(
        _ln_kernel, grid=(t_dim // tm,),
        in_specs=[row, row, vec, vec], out_specs=[row, row],
        out_shape=[jax.ShapeDtypeStruct((t_dim, D_MODEL), F32),
                   jax.ShapeDtypeStruct((t_dim, D_MODEL), BF16)],
        compiler_params=_cparams(("arbitrary",)),
        name="ln",
    )(x, y, g.reshape(1, D_MODEL), b.reshape(1, D_MODEL))


def _pool_kernel(u_ref, h_ref, pw_ref, ps_ref, o_ref, *, ts, tiles_per_seq):
    it = pl.program_id(0) % tiles_per_seq
    x = u_ref[...].astype(F32)
    halo = jnp.where(it == 0, 0.0, h_ref[...].astype(F32))
    xe = jnp.concatenate([halo, x], axis=0)
    t_seq = lax.broadcasted_iota(jnp.int32, (ts, 1), 0) + it * ts
    for g, win in enumerate(POOL_WINDOWS):
        cols = slice(g * POOL_GROUP_DIM, (g + 1) * POOL_GROUP_DIM)
        s = xe[:, cols]
        span = 1
        while span < win:
            s = s + pltpu.roll(s, span, axis=0)
            span *= 2
        cnt = jnp.minimum(t_seq + 1, win).astype(F32)
        d = s[POOL_HALO:, :] / cnt - x[:, cols]
        y = _dot(d.astype(BF16), pw_ref[g].astype(BF16))
        o_ref[:, cols] = (y * ps_ref[:, cols]).astype(o_ref.dtype)


def _pool(main, pool_w, pool_scale, seq):
    t_dim = main.shape[0]
    ts = min(512, seq)
    hb = ts // POOL_HALO
    return pl.pallas_call(
        functools.partial(_pool_kernel, ts=ts, tiles_per_seq=seq // ts),
        grid=(t_dim // ts,),
        in_specs=[pl.BlockSpec((ts, POOL_WIDTH), lambda i: (i, 0)),
                  pl.BlockSpec((POOL_HALO, POOL_WIDTH), lambda i: (jnp.maximum(i * hb - 1, 0), 0)),
                  pl.BlockSpec((4, POOL_GROUP_DIM, POOL_GROUP_DIM), lambda i: (0, 0, 0)),
                  pl.BlockSpec((1, POOL_WIDTH), lambda i: (0, 0))],
        out_specs=pl.BlockSpec((ts, POOL_WIDTH), lambda i: (i, 0)),
        out_shape=jax.ShapeDtypeStruct((t_dim, POOL_WIDTH), BF16),
        compiler_params=_cparams(("arbitrary",)),
        name="pool",
    )(main, main, pool_w, pool_scale.reshape(1, POOL_WIDTH))


def _attn_kernel(q_ref, k_ref, v_ref, r_ref, o_ref, *, tq, tk):
    iq = pl.program_id(2)
    rmat = r_ref[...]

    def blocks(q, row0, js, acc, run, masked):
        n = q.shape[0]
        starts = [pl.multiple_of(j * tk, tk) for j in js]
        z = [_dot(q, k_ref[pl.ds(r0, tk), :], NT) for r0 in starts]
        lk = [jnp.minimum(-zj, 0.0) - jnp.log2(1.0 + jnp.exp2(-jnp.abs(zj))) for zj in z]
        if masked:
            row = lax.broadcasted_iota(jnp.int32, (n, tk), 0) + row0
            causal = [lax.broadcasted_iota(jnp.int32, (n, tk), 1) + j * tk < row for j in js]
            lk = [jnp.where(c, x, 0.0) for c, x in zip(causal, lk)]
        later = [_dot(x.astype(BF16), rmat) for x in lk]
        base = [zj + x + lt for zj, x, lt in zip(z, lk, later)]
        for b, r0 in enumerate(starts):
            att = jnp.exp2(base[b] + run)
            if masked:
                att = jnp.where(causal[b], att, 0.0)
            acc = acc + _dot(att.astype(BF16), v_ref[pl.ds(r0, tk), :])
            run = run + jnp.sum(lk[b], axis=1, keepdims=True)
        return acc, run

    def block(q, row0, j, acc, run, masked):
        return blocks(q, row0, [j], acc, run, masked)

    zero_acc = jnp.zeros((tk, SB_HEAD_DIM), F32)
    zero_run = jnp.zeros((tk, 1), F32)
    q_top, q_bot = q_ref[:tk, :], q_ref[tk:, :]
    row_top = iq * tq
    acc_t, run_t = block(q_top, row_top, 2 * iq, zero_acc, zero_run, True)
    acc_b, run_b = block(q_bot, row_top + tk, 2 * iq + 1, zero_acc, zero_run, True)
    acc_b, run_b = block(q_bot, row_top + tk, 2 * iq, acc_b, run_b, False)
    acc = jnp.concatenate([acc_t, acc_b], axis=0)
    run = jnp.concatenate([run_t, run_b], axis=0)
    q = q_ref[...]

    def pair(n, carry):
        j = 2 * iq - 1 - 2 * n
        return blocks(q, 0, [j, j - 1], *carry, False)

    acc, run = lax.fori_loop(0, iq, pair, (acc, run))
    o_ref[...] = acc.astype(o_ref.dtype)


def _attention(main, batch, seq):
    t_dim = main.shape[0]
    tq = min(512, seq)
    tk = tq // 2
    nq = seq // tq
    q_blk0 = POOL_WIDTH // LANES
    k_blk0 = q_blk0 + SB_HEADS
    v_blk0 = k_blk0 + SB_HEADS
    rmat = (lax.broadcasted_iota(jnp.int32, (tk, tk), 0)
            > lax.broadcasted_iota(jnp.int32, (tk, tk), 1)).astype(BF16)
    return pl.pallas_call(
        functools.partial(_attn_kernel, tq=tq, tk=tk),
        grid=(batch, SB_HEADS, nq),
        in_specs=[pl.BlockSpec((tq, SB_HEAD_DIM), lambda b, h, i: (b * nq + i, q_blk0 + h)),
                  pl.BlockSpec((seq, SB_HEAD_DIM), lambda b, h, i: (b, k_blk0 + h)),
                  pl.BlockSpec((seq, SB_HEAD_DIM), lambda b, h, i: (b, v_blk0 + h)),
                  pl.BlockSpec((tk, tk), lambda b, h, i: (0, 0))],
        out_specs=pl.BlockSpec((tq, SB_HEAD_DIM), lambda b, h, i: (b * nq + i, h)),
        out_shape=jax.ShapeDtypeStruct((t_dim, SB_WIDTH), BF16),
        compiler_params=_cparams(("arbitrary", "arbitrary", "arbitrary")),
        name="attn",
    )(main, main, main, rmat)


def _head_sum(x, bd):
    return jnp.concatenate(
        [_dot32(x[:, c * LANES:(c + 1) * LANES], bd) for c in range(x.shape[1] // LANES)], axis=1)


def _rwkv_prep_kernel(u_ref, h_ref, mu_ref, w0_ref, a0_ref, kk_ref, ka_ref, wc_ref, bd_ref,
                      r_o, k_o, v_o, lw_o, an_o, bn_o, g_o, *, ts, tiles_per_seq):
    it = pl.program_id(0) % tiles_per_seq
    u = u_ref[...]
    last = jnp.where(it == 0, 0.0, h_ref[7:8, :])
    row = lax.broadcasted_iota(jnp.int32, (ts, 1), 0)
    prev = jnp.where(row == 0, last, pltpu.roll(u, 1, axis=0))
    x = u + (prev - u) * mu_ref[...]
    w_dim = RWKV_WIDTH
    r, k, v = x[:, :w_dim], x[:, w_dim:2 * w_dim], x[:, 2 * w_dim:3 * w_dim]
    low = x[:, 3 * w_dim:3 * w_dim + RWKV_LOW]
    lane = lax.broadcasted_iota(jnp.int32, (1, RWKV_LOW), 1)
    low = jnp.where(lane < RWKV_DECAY_RANK, jnp.tanh(low),
                    jnp.where(lane < RWKV_DECAY_RANK + RWKV_A_RANK, low, jax.nn.sigmoid(low)))
    mm = _dot32(low, wc_ref[...])
    w_lin = w0_ref[...] + mm[:, :w_dim]
    soft = jnp.maximum(-w_lin, 0.0) + jnp.log(1.0 + jnp.exp(-jnp.abs(w_lin)))
    log_decay = -jnp.exp(-soft - 0.5)
    a = jax.nn.sigmoid(a0_ref[...] + mm[:, w_dim:2 * w_dim])
    g = mm[:, 2 * w_dim:]
    kk = k * kk_ref[...]
    norm = jnp.sqrt(_head_sum(kk * kk, bd_ref[...]))
    kk = kk / jnp.maximum(norm, 1e-12)
    r_o[...] = r
    k_o[...] = k * (1.0 + (a - 1.0) * ka_ref[...])
    v_o[...] = v
    lw_o[...] = log_decay
    an_o[...] = -kk
    bn_o[...] = kk * a
    g_o[...] = g


def _rwkv_prep(urw, seq, mu, w0, a0, k_k, k_a, wc, bd):
    t_dim = urw.shape[0]
    ts = min(256, seq)
    hb = ts // 8
    wide = pl.BlockSpec((ts, RWKV_PAD), lambda i: (i, 0))
    halo = pl.BlockSpec((8, RWKV_PAD), lambda i: (jnp.maximum(i * hb - 1, 0), 0))
    vec = pl.BlockSpec((1, RWKV_WIDTH), lambda i: (0, 0))
    out = pl.BlockSpec((ts, RWKV_WIDTH), lambda i: (i, 0))
    return pl.pallas_call(
        functools.partial(_rwkv_prep_kernel, ts=ts, tiles_per_seq=seq // ts),
        grid=(t_dim // ts,),
        in_specs=[wide, halo, pl.BlockSpec((1, RWKV_PAD), lambda i: (0, 0)), vec, vec, vec, vec,
                  pl.BlockSpec((RWKV_LOW, 3 * RWKV_WIDTH), lambda i: (0, 0)),
                  pl.BlockSpec((LANES, LANES), lambda i: (0, 0))],
        out_specs=[out] * 7,
        out_shape=[jax.ShapeDtypeStruct((t_dim, RWKV_WIDTH), F32)] * 7,
        compiler_params=_cparams(("arbitrary",), VMEM_BUDGET),
        name="rwkv_prep",
    )(urw, urw, mu, w0, a0, k_k, k_a, wc, bd)


def _rwkv_chunk_kernel(r_ref, k_ref, v_ref, lw_ref, an_ref, bn_ref,
                       m_o, n_o, q_o, y_o, *, pairs):
    c_len = RWKV_CHUNK
    n2 = 2 * c_len
    ri = lax.broadcasted_iota(jnp.int32, (n2, n2), 0)
    ci = lax.broadcasted_iota(jnp.int32, (n2, n2), 1)
    strict = (ri % c_len) > (ci % c_len)
    incl = (ri % c_len) >= (ci % c_len)
    eye = (ri == ci).astype(F32)
    tri = (lax.broadcasted_iota(jnp.int32, (c_len, c_len), 0)
           >= lax.broadcasted_iota(jnp.int32, (c_len, c_len), 1)).astype(F32)
    lane = lax.broadcasted_iota(jnp.int32, (n2, LANES), 1)
    rowh = lax.broadcasted_iota(jnp.int32, (n2, LANES), 0)
    own = (lane < RWKV_HEAD_DIM) == (rowh < c_len)

    def stack(x):
        return jnp.where(own, jnp.concatenate([x, x], axis=0), 0.0)

    cum_all = _dot32(tri, lw_ref[...])
    every = range(pairs)
    sl = [slice(p * LANES, (p + 1) * LANES) for p in every]
    cum = [cum_all[:, s] for s in sl]
    tot = [c[c_len - 1:c_len, :] for c in cum]
    r_t = [stack(r_ref[:, sl[p]] * jnp.exp(cum[p])) for p in every]
    a_t = [stack(an_ref[:, sl[p]] * jnp.exp(cum[p] - lw_ref[:, sl[p]])).astype(BF16) for p in every]
    b_t = [stack(bn_ref[:, sl[p]] * jnp.exp(-cum[p])).astype(BF16) for p in every]
    k_t = [stack(k_ref[:, sl[p]] * jnp.exp(-cum[p])).astype(BF16) for p in every]
    b_e = [stack(bn_ref[:, sl[p]] * jnp.exp(tot[p] - cum[p])).astype(BF16) for p in every]
    k_e = [stack(k_ref[:, sl[p]] * jnp.exp(tot[p] - cum[p])).astype(BF16) for p in every]
    v_s = [stack(v_ref[:, sl[p]]).astype(BF16) for p in every]
    gram = [_dot(jnp.concatenate([a_t[p], r_t[p].astype(BF16)], axis=0),
                 jnp.concatenate([b_t[p], k_t[p]], axis=0), NT) for p in every]
    a_ab = [jnp.where(strict, g[:n2, :n2], 0.0) for g in gram]
    a_ak = [jnp.where(strict, g[:n2, n2:], 0.0).astype(BF16) for g in gram]
    r_ab = [jnp.where(incl, g[n2:, :n2], 0.0).astype(BF16) for g in gram]
    r_ak = [jnp.where(incl, g[n2:, n2:], 0.0).astype(BF16) for g in gram]
    inv = [eye + a for a in a_ab]
    pw = [a.astype(BF16) for a in a_ab]
    pw = [_dot(a, a) for a in pw]
    akv = [_dot(a_ak[p], v_s[p]).astype(BF16) for p in every]
    span = 2
    while 2 * span < c_len:
        pwb = [a.astype(BF16) for a in pw]
        both = [_dot(pwb[p], jnp.concatenate([pwb[p], inv[p].astype(BF16)], axis=1)) for p in every]
        pw = [b[:, :n2] for b in both]
        inv = [inv[p] + both[p][:, n2:] for p in every]
        span *= 2
    inv = [inv[p] + _dot(pw[p].astype(BF16), inv[p].astype(BF16)) for p in every]
    wx = [_dot(inv[p].astype(BF16), jnp.concatenate([a_t[p], akv[p]], axis=1)).astype(BF16)
          for p in every]
    qy = [_dot(r_ab[p], wx[p]) for p in every]
    rv = [_dot(r_ak[p], v_s[p]) for p in every]
    mn = [_dot(b_e[p], wx[p], TN) for p in every]
    kv = [_dot(k_e[p], v_s[p], TN) for p in every]
    for p in every:
        q_s = r_t[p] + qy[p][:, :n2]
        y_s = qy[p][:, n2:] + rv[p]
        m_o[0, p] = eye * jnp.exp(tot[p]) + mn[p][:, :n2]
        n_o[0, p] = mn[p][:, n2:] + kv[p]
        q_o[:, sl[p]] = q_s[:c_len] + q_s[c_len:]
        y_o[:, sl[p]] = y_s[:c_len] + y_s[c_len:]


def _rwkv_chunks(r, k, v, lw, an, bn):
    t_dim = r.shape[0]
    pairs = 4
    n_chunks = t_dim // RWKV_CHUNK
    tile = pl.BlockSpec((RWKV_CHUNK, pairs * LANES), lambda c, p: (c, p))
    op = pl.BlockSpec((1, pairs, LANES, LANES), lambda c, p: (c, p, 0, 0))
    op_shape = jax.ShapeDtypeStruct((n_chunks, RWKV_PAIRS, LANES, LANES), F32)
    row_shape = jax.ShapeDtypeStruct((t_dim, RWKV_WIDTH), F32)
    return pl.pallas_call(
        functools.partial(_rwkv_chunk_kernel, pairs=pairs),
        grid=(n_chunks, RWKV_PAIRS // pairs),
        in_specs=[tile] * 6,
        out_specs=[op, op, tile, tile],
        out_shape=[op_shape, op_shape, row_shape, row_shape],
        compiler_params=_cparams(("arbitrary", "arbitrary")),
        name="rwkv_chunk",
    )(r, k, v, lw, an, bn)


def _rwkv_sweep_kernel(m_ref, n_ref, q_ref, y0_ref, y_o, z_ref, *, pairs, chunks):
    @pl.when(pl.program_id(1) == 0)
    def _():
        z_ref[...] = jnp.zeros_like(z_ref)

    sl = [slice(p * LANES, (p + 1) * LANES) for p in range(pairs)]
    z = [z_ref[p] for p in range(pairs)]
    for c in range(chunks):
        rows = slice(c * RWKV_CHUNK, (c + 1) * RWKV_CHUNK)
        zs = [_split(zp) for zp in z]
        ys = [_dot3s(q_ref[rows, sl[p]], zs[p]) for p in range(pairs)]
        zn = [_dot3s(m_ref[c, p], zs[p]) for p in range(pairs)]
        for p in range(pairs):
            y_o[rows, sl[p]] = ys[p] + y0_ref[rows, sl[p]]
        z = [zn[p] + n_ref[c, p] for p in range(pairs)]
    for p in range(pairs):
        z_ref[p] = z[p]


def _rwkv_sweep(m_op, n_op, q_op, y0, batch, seq):
    t_dim = q_op.shape[0]
    pairs = RWKV_PAIRS
    chunks = min(4, seq // RWKV_CHUNK)
    steps = seq // (RWKV_CHUNK * chunks)
    tile = pl.BlockSpec((RWKV_CHUNK * chunks, pairs * LANES), lambda b, s: (b * steps + s, 0))
    op = pl.BlockSpec((chunks, pairs, LANES, LANES), lambda b, s: (b * steps + s, 0, 0, 0))
    return pl.pallas_call(
        functools.partial(_rwkv_sweep_kernel, pairs=pairs, chunks=chunks),
        grid=(batch, steps),
        in_specs=[op, op, tile, tile],
        out_specs=tile,
        out_shape=jax.ShapeDtypeStruct((t_dim, RWKV_WIDTH), F32),
        scratch_shapes=[pltpu.VMEM((pairs, LANES, LANES), F32)],
        compiler_params=_cparams(("arbitrary", "arbitrary")),
        name="rwkv_sweep",
    )(m_op, n_op, q_op, y0)


def _rwkv_out_kernel(y_ref, r_ref, k_ref, v_ref, g_ref, rk_ref, lg_ref, lb_ref, bd_ref, o_ref):
    bd = bd_ref[...]
    y = y_ref[...]
    mu = _head_sum(y, bd) * (1.0 / RWKV_HEAD_DIM)
    yc = y - mu
    var = _head_sum(yc * yc, bd) * (1.0 / RWKV_HEAD_DIM)
    yn = yc * lax.rsqrt(var + RWKV_GN_EPS) * lg_ref[...] + lb_ref[...]
    bonus = _head_sum(r_ref[...] * k_ref[...] * rk_ref[...], bd) * v_ref[...]
    o_ref[...] = ((yn + bonus) * g_ref[...]).astype(o_ref.dtype)


def _rwkv_out(y, r, k, v, g, r_k, ln_g, ln_b, bd):
    t_dim = y.shape[0]
    ts = min(256, t_dim)
    tile = pl.BlockSpec((ts, RWKV_WIDTH), lambda i: (i, 0))
    vec = pl.BlockSpec((1, RWKV_WIDTH), lambda i: (0, 0))
    return pl.pallas_call(
        _rwkv_out_kernel, grid=(t_dim // ts,),
        in_specs=[tile] * 5 + [vec] * 3 + [pl.BlockSpec((LANES, LANES), lambda i: (0, 0))],
        out_specs=tile,
        out_shape=jax.ShapeDtypeStruct((t_dim, RWKV_WIDTH), BF16),
        compiler_params=_cparams(("arbitrary",)),
        name="rwkv_out",
    )(y, r, k, v, g, r_k, ln_g, ln_b, bd)


def _merge_kernel(yp_ref, ya_ref, yr_ref, wp_ref, wa_ref, wr_ref, g0_ref, g1_ref, g2_ref,
                  b0_ref, b1_ref, b2_ref, o_ref, wpb, wab, wrb):
    @pl.when(pl.program_id(1) == 0)
    def _():
        wpb[...] = wp_ref[...].astype(BF16)
        wab[...] = wa_ref[...].astype(BF16)
        wrb[...] = wr_ref[...].astype(BF16)

    m = jax.nn.sigmoid(g0_ref[...] + b0_ref[...]) * _dot(yp_ref[...], wpb[...])
    m = m + jax.nn.sigmoid(g1_ref[...] + b1_ref[...]) * _dot(ya_ref[...], wab[...])
    m = m + jax.nn.sigmoid(g2_ref[...] + b2_ref[...]) * _dot(yr_ref[...], wrb[...])
    o_ref[...] = m.astype(o_ref.dtype)


def _merge(y_pool, y_attn, y_rwkv, w_pool, w_attn, w_rwkv, layer, gate_logits, b_gate):
    t_dim = y_pool.shape[0]
    tm = min(512, t_dim)
    tn = 512
    nb = D_MODEL // tn
    act = lambda width: pl.BlockSpec((tm, width), lambda j, i: (i, 0))
    wgt = lambda width: pl.BlockSpec((None, width, tn), lambda j, i: (layer, 0, j))
    gate = lambda br: pl.BlockSpec((tm, tn), lambda j, i: (i, br * nb + j))
    bias = lambda br: pl.BlockSpec((1, tn), lambda j, i: (0, br * nb + j))
    return pl.pallas_call(
        _merge_kernel, grid=(nb, t_dim // tm),
        in_specs=[act(POOL_WIDTH), act(SB_WIDTH), act(RWKV_WIDTH),
                  wgt(POOL_WIDTH), wgt(SB_WIDTH), wgt(RWKV_WIDTH),
                  gate(0), gate(1), gate(2), bias(0), bias(1), bias(2)],
        out_specs=pl.BlockSpec((tm, tn), lambda j, i: (i, j)),
        out_shape=jax.ShapeDtypeStruct((t_dim, D_MODEL), BF16),
        scratch_shapes=[pltpu.VMEM((POOL_WIDTH, tn), BF16), pltpu.VMEM((SB_WIDTH, tn), BF16),
                        pltpu.VMEM((RWKV_WIDTH, tn), BF16)],
        compiler_params=_cparams(("arbitrary", "arbitrary"), VMEM_BUDGET),
        name="merge",
    )(y_pool, y_attn, y_rwkv, w_pool, w_attn, w_rwkv, gate_logits, gate_logits, gate_logits,
      b_gate, b_gate, b_gate)


def _ffn_up_kernel(a_ref, wa_ref, wl_ref, cw_ref, cb_ref, o_ref, wb_ref, halo_ref, *, tiles_per_seq):
    i = pl.program_id(1)
    tm, tn = o_ref.shape

    @pl.when(i == 0)
    def _():
        _cast_panel([wa_ref], wb_ref.at[:, :tn], 0, tn)
        _cast_panel([wl_ref], wb_ref.at[:, tn:], 0, tn)
        halo_ref[...] = jnp.zeros_like(halo_ref)

    prev = jnp.where(i % tiles_per_seq == 0, 0.0, halo_ref[...])
    row = lax.broadcasted_iota(jnp.int32, (DOT_ROWS, 1), 0)
    for c in range(tm // DOT_ROWS):
        rows = slice(c * DOT_ROWS, (c + 1) * DOT_ROWS)
        res = _dot(a_ref[rows, :], wb_ref[...])
        act, lin = res[:, :tn], res[:, tn:]
        p1 = jnp.where(row == 0, prev[7:8, :], pltpu.roll(act, 1, axis=0))
        p2 = jnp.where(row == 0, prev[6:7, :],
                       jnp.where(row == 1, prev[7:8, :], pltpu.roll(act, 2, axis=0)))
        cv = p2 * cw_ref[0:1, :] + p1 * cw_ref[1:2, :] + act * cw_ref[2:3, :] + cb_ref[...]
        gelu = 0.5 * cv * (1.0 + lax.erf(cv * (2.0 ** -0.5)))
        o_ref[rows, :] = (gelu * lin).astype(o_ref.dtype)
        prev = act[DOT_ROWS - 8:, :]
    halo_ref[...] = prev


def _ffn_up(a, w_up, layer, conv_w, conv_b, seq):
    t_dim, k_dim = a.shape
    tm = min(1024, seq)
    tn = 256
    nb = D_FF // tn
    return pl.pallas_call(
        functools.partial(_ffn_up_kernel, tiles_per_seq=seq // tm),
        grid=(nb, t_dim // tm),
        in_specs=[pl.BlockSpec((tm, k_dim), lambda j, i: (i, 0)),
                  pl.BlockSpec((None, k_dim, tn), lambda j, i: (layer, 0, j)),
                  pl.BlockSpec((None, k_dim, tn), lambda j, i: (layer, 0, nb + j)),
                  pl.BlockSpec((3, tn), lambda j, i: (0, j)),
                  pl.BlockSpec((1, tn), lambda j, i: (0, j))],
        out_specs=pl.BlockSpec((tm, tn), lambda j, i: (i, j)),
        out_shape=jax.ShapeDtypeStruct((t_dim, D_FF), BF16),
        scratch_shapes=[pltpu.VMEM((k_dim, 2 * tn), BF16), pltpu.VMEM((8, tn), F32)],
        compiler_params=_cparams(("arbitrary", "arbitrary"), VMEM_BUDGET),
        name="ffn_up",
    )(a, w_up, w_up, conv_w, conv_b.reshape(1, D_FF))


def kernel(x, w_in, b_gate, pool_w, pool_scale, rwkv_mu, rwkv_w0, rwkv_w2, rwkv_a0, rwkv_a2, rwkv_g2, rwkv_k_k, rwkv_k_a, rwkv_r_k, rwkv_ln_g, rwkv_ln_b, w_branch_pool, w_branch_attn, w_branch_rwkv, w_out, ln1_g, ln1_b, w_up, ffn_conv_w, ffn_conv_b, w_down, ln2_g, ln2_b):
    batch, seq, _ = x.shape
    t_dim = batch * seq
    xf = x.reshape(t_dim, D_MODEL)
    xb = xf.astype(BF16)
    half = lax.broadcasted_iota(jnp.int32, (LANES, LANES), 0) // RWKV_HEAD_DIM
    bd = (half == half.T).astype(F32)
    vec = lambda p: p.reshape(1, -1)
    col = lax.broadcasted_iota(jnp.int32, (1, N_MAIN), 1)
    q_cols = (col >= POOL_WIDTH) & (col < POOL_WIDTH + SB_WIDTH)
    main_scale = jnp.where(q_cols, SB_HEAD_DIM ** -0.5 * math.log2(math.e), 1.0).astype(F32)
    for l in range(DEPTH):
        main = _mm(xb, w_in, l, col0=0, n_cols=N_MAIN, out_dtype=BF16, col_scale=main_scale)
        urw = _mm(xb, w_in, l, col0=N_MAIN, n_cols=RWKV_PAD, out_dtype=F32)
        gate_logits = _mm(xb, w_in, l, col0=GATE_OFF, n_cols=3 * D_MODEL, out_dtype=F32)

        y_pool = _pool(main, pool_w[l], pool_scale[l], seq)
        y_attn = _attention(main, batch, seq)

        wc = jnp.zeros((RWKV_LOW, 3 * RWKV_WIDTH), F32)
        wc = wc.at[:RWKV_DECAY_RANK, :RWKV_WIDTH].set(rwkv_w2[l])
        wc = wc.at[RWKV_DECAY_RANK:RWKV_DECAY_RANK + RWKV_A_RANK, RWKV_WIDTH:2 * RWKV_WIDTH].set(rwkv_a2[l])
        lo = RWKV_DECAY_RANK + RWKV_A_RANK
        wc = wc.at[lo:lo + RWKV_GATE_RANK, 2 * RWKV_WIDTH:].set(rwkv_g2[l])
        mu = jnp.pad(rwkv_mu[l], (0, RWKV_PAD - RWKV_SHIFT_WIDTH)).reshape(1, RWKV_PAD)
        r, k, v, lw, an, bn, g = _rwkv_prep(urw, seq, mu, vec(rwkv_w0[l]), vec(rwkv_a0[l]),
                                            vec(rwkv_k_k[l]), vec(rwkv_k_a[l]), wc, bd)
        m_op, n_op, q_op, y0 = _rwkv_chunks(r, k, v, lw, an, bn)
        y_scan = _rwkv_sweep(m_op, n_op, q_op, y0, batch, seq)
        y_rwkv = _rwkv_out(y_scan, r, k, v, g, vec(rwkv_r_k[l]), vec(rwkv_ln_g[l]),
                           vec(rwkv_ln_b[l]), bd)

        merged = _merge(y_pool, y_attn, y_rwkv, w_branch_pool, w_branch_attn, w_branch_rwkv, l,
                        gate_logits, vec(b_gate[l]))
        sub = _mm(merged, w_out, l, col0=0, n_cols=D_MODEL, out_dtype=F32)
        xf, xb = _ln(xf, sub, ln1_g[l], ln1_b[l])

        h = _ffn_up(xb, w_up, l, ffn_conv_w[l], ffn_conv_b[l], seq)
        ffn = _mm(h, w_down, l, col0=0, n_cols=D_MODEL, out_dtype=F32, tm=512, tn=256)
        xf, xb = _ln(xf, ffn, ln2_g[l], ln2_b[l])
    return xf.reshape(batch, seq, D_MODEL)
```

```python
import functools
import math

import jax
import jax.numpy as jnp
from jax import lax
from jax.experimental import pallas as pl
from jax.experimental.pallas import tpu as pltpu

F32 = jnp.float32
BF16 = jnp.bfloat16

D_MODEL = 4096
DEPTH = 2
POOL_WIDTH = 1024
POOL_WINDOWS = (2, 4, 8, 16)
POOL_GROUP_DIM = 256
POOL_HALO = 16
SB_HEAD_DIM = 128
SB_WIDTH = 2048
SB_HEADS = 16
RWKV_WIDTH = 1024
RWKV_HEAD_DIM = 64
RWKV_DECAY_RANK = 64
RWKV_A_RANK = 64
RWKV_GATE_RANK = 160
RWKV_SHIFT_WIDTH = 3 * RWKV_WIDTH + RWKV_DECAY_RANK + RWKV_A_RANK + RWKV_GATE_RANK
RWKV_GN_EPS = 64e-5
N_MAIN = POOL_WIDTH + 3 * SB_WIDTH
RWKV_PAD = 3584
RWKV_LOW = 384
GATE_OFF = N_MAIN + RWKV_SHIFT_WIDTH
D_FF = 11008
DEEPNORM_ALPHA = (2 * DEPTH) ** 0.25
LN_EPS = 1e-5

LANES = 128
RWKV_CHUNK = 64
RWKV_PAIRS = RWKV_WIDTH // LANES
VMEM_BUDGET = 58 * 1024 * 1024
CAST_ROWS = 256
DOT_ROWS = 256
DOT_ROWS_T = 512


def _cparams(sem, vmem=None):
    return pltpu.CompilerParams(dimension_semantics=sem, vmem_limit_bytes=vmem)


def _dot(a, b, dims=None, prec=None):
    if dims is None:
        dims = (((a.ndim - 1,), (0,)), ((), ()))
    return lax.dot_general(a, b, dims, precision=prec, preferred_element_type=F32)


def _dot32(a, b, dims=None):
    return _dot(a, b, dims, prec=lax.Precision.HIGHEST)


def _split(a):
    hi = a.astype(BF16)
    return hi, (a - hi.astype(F32)).astype(BF16)


def _dot3s(a, b_split):
    ah, al = _split(a)
    bh, bl = b_split
    return _dot(ah, bh) + _dot(ah, bl) + _dot(al, bh)


NT = (((1,), (1,)), ((), ()))
TN = (((0,), (0,)), ((), ()))


def _cast_panel(w_ref, wb_ref):
    def cast(c, carry):
        r = pl.multiple_of(c * CAST_ROWS, CAST_ROWS)
        wb_ref[pl.ds(r, CAST_ROWS), :] = w_ref[pl.ds(r, CAST_ROWS), :].astype(BF16)
        return carry
    lax.fori_loop(0, wb_ref.shape[0] // CAST_ROWS, cast, 0)


def _mm_kernel(a_ref, w_ref, o_ref, wb_ref):
    tm = o_ref.shape[0]

    @pl.when(pl.program_id(1) == 0)
    def _():
        _cast_panel(w_ref, wb_ref)

    for c in range(tm // DOT_ROWS):
        rows = slice(c * DOT_ROWS, (c + 1) * DOT_ROWS)
        o_ref[rows, :] = _dot(a_ref[rows, :], wb_ref[...]).astype(o_ref.dtype)


def _mm(a, w, layer, *, out_dtype, tm, tn, single_buffer_w=False):
    m_dim, k_dim = a.shape
    n_cols = w.shape[2]
    tm = min(tm, m_dim)
    w_mode = dict(pipeline_mode=pl.Buffered(1)) if single_buffer_w else {}
    return pl.pallas_call(
        _mm_kernel,
        grid=(n_cols // tn, m_dim // tm),
        in_specs=[pl.BlockSpec((tm, k_dim), lambda j, i: (i, 0)),
                  pl.BlockSpec((None, k_dim, tn), lambda j, i: (layer, 0, j), **w_mode)],
        out_specs=pl.BlockSpec((tm, tn), lambda j, i: (i, j)),
        out_shape=jax.ShapeDtypeStruct((m_dim, n_cols), out_dtype),
        scratch_shapes=[pltpu.VMEM((k_dim, tn), BF16)],
        compiler_params=_cparams(("arbitrary", "arbitrary"), VMEM_BUDGET),
        name="mm",
    )(a, w)


def _mm_t_kernel(*refs, scaled):
    a_ref, w_ref = refs[0], refs[1]
    s_ref = refs[2] if scaled else None
    o_ref, wb_ref = refs[-2], refs[-1]
    tm, tn = o_ref.shape

    @pl.when(pl.program_id(1) == 0)
    def _():
        _cast_panel(w_ref, wb_ref)

    for c in range(tm // DOT_ROWS_T):
        rows = slice(c * DOT_ROWS_T, (c + 1) * DOT_ROWS_T)
        res = _dot(a_ref[rows, :], wb_ref[...], NT)
        if scaled:
            res = res * s_ref[...]
        o_ref[rows, :] = res.astype(o_ref.dtype)


def _mm_t(a, wt, layer, *, row0, n_cols, out_dtype, tm=1024, tn=512, col_scale=None):
    m_dim, k_dim = a.shape
    tm = min(tm, m_dim)
    operands = [a, wt]
    in_specs = [pl.BlockSpec((tm, k_dim), lambda j, i: (i, 0)),
                pl.BlockSpec((None, pl.Element(tn), pl.Element(k_dim)),
                             lambda j, i: (layer, pl.multiple_of(row0 + j * tn, 32), 0))]
    if col_scale is not None:
        operands.append(col_scale)
        in_specs.append(pl.BlockSpec((1, tn), lambda j, i: (0, j)))
    return pl.pallas_call(
        functools.partial(_mm_t_kernel, scaled=col_scale is not None),
        grid=(n_cols // tn, m_dim // tm),
        in_specs=in_specs,
        out_specs=pl.BlockSpec((tm, tn), lambda j, i: (i, j)),
        out_shape=jax.ShapeDtypeStruct((m_dim, n_cols), out_dtype),
        scratch_shapes=[pltpu.VMEM((tn, k_dim), BF16)],
        compiler_params=_cparams(("arbitrary", "arbitrary"), VMEM_BUDGET),
        name="mm_t",
    )(*operands)


def _ln_kernel(x_ref, y_ref, g_ref, b_ref, of_ref, ob_ref):
    z = DEEPNORM_ALPHA * x_ref[...] + y_ref[...]
    mu = jnp.mean(z, axis=-1, keepdims=True)
    zc = z - mu
    var = jnp.mean(zc * zc, axis=-1, keepdims=True)
    out = zc * lax.rsqrt(var + LN_EPS) * g_ref[...] + b_ref[...]
    of_ref[...] = out
    ob_ref[...] = out.astype(BF16)


def _ln(x, y, g, b):
    t_dim = x.shape[0]
    tm = min(128, t_dim)
    row = pl.BlockSpec((tm, D_MODEL), lambda i: (i, 0))
    vec = pl.BlockSpec((1, D_MODEL), lambda i: (0, 0))
    return pl.pallas_call(
        _ln_kernel, grid=(t_dim // tm,),
        in_specs=[row, row, vec, vec], out_specs=[row, row],
        out_shape=[jax.ShapeDtypeStruct((t_dim, D_MODEL), F32),
                   jax.ShapeDtypeStruct((t_dim, D_MODEL), BF16)],
        compiler_params=_cparams(("arbitrary",)),
        name="ln",
    )(x, y, g.reshape(1, D_MODEL), b.reshape(1, D_MODEL))


def _pool_kernel(u_ref, h_ref, pw_ref, ps_ref, o_ref, *, ts, tiles_per_seq):
    it = pl.program_id(0) % tiles_per_seq
    x = u_ref[...].astype(F32)
    halo = jnp.where(it == 0, 0.0, h_ref[...].astype(F32))
    xe = jnp.concatenate([halo, x], axis=0)
    t_seq = lax.broadcasted_iota(jnp.int32, (ts, 1), 0) + it * ts
    for g, win in enumerate(POOL_WINDOWS):
        cols = slice(g * POOL_GROUP_DIM, (g + 1) * POOL_GROUP_DIM)
        s = xe[:, cols]
        span = 1
        while span < win:
            s = s + pltpu.roll(s, span, axis=0)
            span *= 2
        cnt = jnp.minimum(t_seq + 1, win).astype(F32)
        d = s[POOL_HALO:, :] / cnt - x[:, cols]
        y = _dot(d.astype(BF16), pw_ref[g].astype(BF16))
        o_ref[:, cols] = (y * ps_ref[:, cols]).astype(o_ref.dtype)


def _pool(main, pool_w, pool_scale, seq):
    t_dim = main.shape[0]
    ts = min(512, seq)
    hb = ts // POOL_HALO
    return pl.pallas_call(
        functools.partial(_pool_kernel, ts=ts, tiles_per_seq=seq // ts),
        grid=(t_dim // ts,),
        in_specs=[pl.BlockSpec((ts, POOL_WIDTH), lambda i: (i, 0)),
                  pl.BlockSpec((POOL_HALO, POOL_WIDTH), lambda i: (jnp.maximum(i * hb - 1, 0), 0)),
                  pl.BlockSpec((4, POOL_GROUP_DIM, POOL_GROUP_DIM), lambda i: (0, 0, 0)),
                  pl.BlockSpec((1, POOL_WIDTH), lambda i: (0, 0))],
        out_specs=pl.BlockSpec((ts, POOL_WIDTH), lambda i: (i, 0)),
        out_shape=jax.ShapeDtypeStruct((t_dim, POOL_WIDTH), BF16),
        compiler_params=_cparams(("arbitrary",)),
        name="pool",
    )(main, main, pool_w, pool_scale.reshape(1, POOL_WIDTH))


def _attn_kernel(q_ref, k_ref, v_ref, r_ref, o_ref, *, tq, tk):
    iq = pl.program_id(2)
    rmat = r_ref[...]

    def blocks(q, row0, js, acc, run, masked):
        n = q.shape[0]
        starts = [pl.multiple_of(j * tk, tk) for j in js]
        z = [_dot(q, k_ref[pl.ds(r0, tk), :], NT) for r0 in starts]
        lk = [jnp.minimum(-zj, 0.0) - jnp.log2(1.0 + jnp.exp2(-jnp.abs(zj))) for zj in z]
        if masked:
            row = lax.broadcasted_iota(jnp.int32, (n, tk), 0) + row0
            causal = [lax.broadcasted_iota(jnp.int32, (n, tk), 1) + j * tk < row for j in js]
            lk = [jnp.where(c, x, 0.0) for c, x in zip(causal, lk)]
        later = [_dot(x.astype(BF16), rmat) for x in lk]
        base = [zj + x + lt for zj, x, lt in zip(z, lk, later)]
        for b, r0 in enumerate(starts):
            att = jnp.exp2(base[b] + run)
            if masked:
                att = jnp.where(causal[b], att, 0.0)
            acc = acc + _dot(att.astype(BF16), v_ref[pl.ds(r0, tk), :])
            run = run + jnp.sum(lk[b], axis=1, keepdims=True)
        return acc, run

    def block(q, row0, j, acc, run, masked):
        return blocks(q, row0, [j], acc, run, masked)

    zero_acc = jnp.zeros((tk, SB_HEAD_DIM), F32)
    zero_run = jnp.zeros((tk, 1), F32)
    q_top, q_bot = q_ref[:tk, :], q_ref[tk:, :]
    row_top = iq * tq
    acc_t, run_t = block(q_top, row_top, 2 * iq, zero_acc, zero_run, True)
    acc_b, run_b = block(q_bot, row_top + tk, 2 * iq + 1, zero_acc, zero_run, True)
    acc_b, run_b = block(q_bot, row_top + tk, 2 * iq, acc_b, run_b, False)
    acc = jnp.concatenate([acc_t, acc_b], axis=0)
    run = jnp.concatenate([run_t, run_b], axis=0)
    q = q_ref[...]

    def pair(n, carry):
        j = 2 * iq - 1 - 2 * n
        return blocks(q, 0, [j, j - 1], *carry, False)

    acc, run = lax.fori_loop(0, iq, pair, (acc, run))
    o_ref[...] = acc.astype(o_ref.dtype)


def _attention(main, batch, seq):
    t_dim = main.shape[0]
    tq = min(512, seq)
    tk = tq // 2
    nq = seq // tq
    q_blk0 = POOL_WIDTH // LANES
    k_blk0 = q_blk0 + SB_HEADS
    v_blk0 = k_blk0 + SB_HEADS
    rmat = (lax.broadcasted_iota(jnp.int32, (tk, tk), 0)
            > lax.broadcasted_iota(jnp.int32, (tk, tk), 1)).astype(BF16)
    return pl.pallas_call(
        functools.partial(_attn_kernel, tq=tq, tk=tk),
        grid=(batch, SB_HEADS, nq),
        in_specs=[pl.BlockSpec((tq, SB_HEAD_DIM), lambda b, h, i: (b * nq + i, q_blk0 + h)),
                  pl.BlockSpec((seq, SB_HEAD_DIM), lambda b, h, i: (b, k_blk0 + h)),
                  pl.BlockSpec((seq, SB_HEAD_DIM), lambda b, h, i: (b, v_blk0 + h)),
                  pl.BlockSpec((tk, tk), lambda b, h, i: (0, 0))],
        out_specs=pl.BlockSpec((tq, SB_HEAD_DIM), lambda b, h, i: (b * nq + i, h)),
        out_shape=jax.ShapeDtypeStruct((t_dim, SB_WIDTH), BF16),
        compiler_params=_cparams(("arbitrary", "arbitrary", "arbitrary")),
        name="attn",
    )(main, main, main, rmat)


def _head_sum(x, bd):
    hi, lo = _split(x)
    return jnp.concatenate(
        [_dot(hi[:, c * LANES:(c + 1) * LANES], bd) + _dot(lo[:, c * LANES:(c + 1) * LANES], bd)
         for c in range(x.shape[1] // LANES)], axis=1)


def _rwkv_prep_kernel(u_ref, h_ref, mu_ref, w0_ref, a0_ref, kk_ref, ka_ref, wch_ref, wcl_ref, bd_ref,
                      r_o, k_o, v_o, lw_o, an_o, bn_o, g_o, *, ts, tiles_per_seq):
    it = pl.program_id(0) % tiles_per_seq
    u = u_ref[...]
    last = jnp.where(it == 0, 0.0, h_ref[7:8, :])
    row = lax.broadcasted_iota(jnp.int32, (ts, 1), 0)
    prev = jnp.where(row == 0, last, pltpu.roll(u, 1, axis=0))
    x = u + (prev - u) * mu_ref[...]
    w_dim = RWKV_WIDTH
    r, k, v = x[:, :w_dim], x[:, w_dim:2 * w_dim], x[:, 2 * w_dim:3 * w_dim]
    low = x[:, 3 * w_dim:3 * w_dim + RWKV_LOW]
    lane = lax.broadcasted_iota(jnp.int32, (1, RWKV_LOW), 1)
    low = jnp.where(lane < RWKV_DECAY_RANK, jnp.tanh(low),
                    jnp.where(lane < RWKV_DECAY_RANK + RWKV_A_RANK, low, jax.nn.sigmoid(low)))
    mm = _dot3s(low, (wch_ref[...], wcl_ref[...]))
    w_lin = w0_ref[...] + mm[:, :w_dim]
    soft = jnp.maximum(-w_lin, 0.0) + jnp.log(1.0 + jnp.exp(-jnp.abs(w_lin)))
    log_decay = -jnp.exp(-soft - 0.5)
    a = jax.nn.sigmoid(a0_ref[...] + mm[:, w_dim:2 * w_dim])
    g = mm[:, 2 * w_dim:]
    kk = k * kk_ref[...]
    norm = jnp.sqrt(_head_sum(kk * kk, bd_ref[...]))
    kk = kk / jnp.maximum(norm, 1e-12)
    r_o[...] = r
    k_o[...] = k * (1.0 + (a - 1.0) * ka_ref[...])
    v_o[...] = v
    lw_o[...] = log_decay
    an_o[...] = -kk
    bn_o[...] = kk * a
    g_o[...] = g


def _rwkv_prep(urw, seq, mu, w0, a0, k_k, k_a, wc, bd):
    t_dim = urw.shape[0]
    wc_hi, wc_lo = _split(wc)
    ts = min(256, seq)
    hb = ts // 8
    wide = pl.BlockSpec((ts, RWKV_PAD), lambda i: (i, 0))
    halo = pl.BlockSpec((8, RWKV_PAD), lambda i: (jnp.maximum(i * hb - 1, 0), 0))
    vec = pl.BlockSpec((1, RWKV_WIDTH), lambda i: (0, 0))
    out = pl.BlockSpec((ts, RWKV_WIDTH), lambda i: (i, 0))
    return pl.pallas_call(
        functools.partial(_rwkv_prep_kernel, ts=ts, tiles_per_seq=seq // ts),
        grid=(t_dim // ts,),
        in_specs=[wide, halo, pl.BlockSpec((1, RWKV_PAD), lambda i: (0, 0)), vec, vec, vec, vec,
                  pl.BlockSpec((RWKV_LOW, 3 * RWKV_WIDTH), lambda i: (0, 0)),
                  pl.BlockSpec((RWKV_LOW, 3 * RWKV_WIDTH), lambda i: (0, 0)),
                  pl.BlockSpec((LANES, LANES), lambda i: (0, 0))],
        out_specs=[out] * 7,
        out_shape=[jax.ShapeDtypeStruct((t_dim, RWKV_WIDTH), F32)] * 7,
        compiler_params=_cparams(("arbitrary",), VMEM_BUDGET),
        name="rwkv_prep",
    )(urw, urw, mu, w0, a0, k_k, k_a, wc_hi, wc_lo, bd)


def _rwkv_chunk_kernel(r_ref, k_ref, v_ref, lw_ref, an_ref, bn_ref,
                       m_o, n_o, q_o, y_o, *, pairs):
    c_len = RWKV_CHUNK
    n2 = 2 * c_len
    ri = lax.broadcasted_iota(jnp.int32, (n2, n2), 0)
    ci = lax.broadcasted_iota(jnp.int32, (n2, n2), 1)
    strict = (ri % c_len) > (ci % c_len)
    incl = (ri % c_len) >= (ci % c_len)
    eye = (ri == ci).astype(F32)
    tri = (lax.broadcasted_iota(jnp.int32, (c_len, c_len), 0)
           >= lax.broadcasted_iota(jnp.int32, (c_len, c_len), 1)).astype(F32)
    lane = lax.broadcasted_iota(jnp.int32, (n2, LANES), 1)
    rowh = lax.broadcasted_iota(jnp.int32, (n2, LANES), 0)
    own = (lane < RWKV_HEAD_DIM) == (rowh < c_len)

    def stack(x):
        return jnp.where(own, jnp.concatenate([x, x], axis=0), 0.0)

    cum_all = _dot32(tri, lw_ref[...])
    every = range(pairs)
    sl = [slice(p * LANES, (p + 1) * LANES) for p in every]
    cum = [cum_all[:, s] for s in sl]
    tot = [c[c_len - 1:c_len, :] for c in cum]
    r_t = [stack(r_ref[:, sl[p]] * jnp.exp(cum[p])) for p in every]
    a_t = [stack(an_ref[:, sl[p]] * jnp.exp(cum[p] - lw_ref[:, sl[p]])).astype(BF16) for p in every]
    b_t = [stack(bn_ref[:, sl[p]] * jnp.exp(-cum[p])).astype(BF16) for p in every]
    k_t = [stack(k_ref[:, sl[p]] * jnp.exp(-cum[p])).astype(BF16) for p in every]
    b_e = [stack(bn_ref[:, sl[p]] * jnp.exp(tot[p] - cum[p])).astype(BF16) for p in every]
    k_e = [stack(k_ref[:, sl[p]] * jnp.exp(tot[p] - cum[p])).astype(BF16) for p in every]
    v_s = [stack(v_ref[:, sl[p]]).astype(BF16) for p in every]
    gram = [_dot(jnp.concatenate([a_t[p], r_t[p].astype(BF16)], axis=0),
                 jnp.concatenate([b_t[p], k_t[p]], axis=0), NT) for p in every]
    a_ab = [jnp.where(strict, g[:n2, :n2], 0.0) for g in gram]
    a_ak = [jnp.where(strict, g[:n2, n2:], 0.0).astype(BF16) for g in gram]
    r_ab = [jnp.where(incl, g[n2:, :n2], 0.0).astype(BF16) for g in gram]
    r_ak = [jnp.where(incl, g[n2:, n2:], 0.0).astype(BF16) for g in gram]
    inv = [eye + a for a in a_ab]
    pw = [a.astype(BF16) for a in a_ab]
    pw = [_dot(a, a) for a in pw]
    akv = [_dot(a_ak[p], v_s[p]).astype(BF16) for p in every]
    span = 2
    while 2 * span < c_len:
        pwb = [a.astype(BF16) for a in pw]
        both = [_dot(pwb[p], jnp.concatenate([pwb[p], inv[p].astype(BF16)], axis=1)) for p in every]
        pw = [b[:, :n2] for b in both]
        inv = [inv[p] + both[p][:, n2:] for p in every]
        span *= 2
    inv = [inv[p] + _dot(pw[p].astype(BF16), inv[p].astype(BF16)) for p in every]
    wx = [_dot(inv[p].astype(BF16), jnp.concatenate([a_t[p], akv[p]], axis=1)).astype(BF16)
          for p in every]
    qy = [_dot(r_ab[p], wx[p]) for p in every]
    rv = [_dot(r_ak[p], v_s[p]) for p in every]
    mn = [_dot(b_e[p], wx[p], TN) for p in every]
    kv = [_dot(k_e[p], v_s[p], TN) for p in every]
    for p in every:
        q_s = r_t[p] + qy[p][:, :n2]
        y_s = qy[p][:, n2:] + rv[p]
        m_o[0, p] = eye * jnp.exp(tot[p]) + mn[p][:, :n2]
        n_o[0, p] = mn[p][:, n2:] + kv[p]
        q_o[:, sl[p]] = q_s[:c_len] + q_s[c_len:]
        y_o[:, sl[p]] = y_s[:c_len] + y_s[c_len:]


def _rwkv_chunks(r, k, v, lw, an, bn):
    t_dim = r.shape[0]
    pairs = RWKV_PAIRS
    n_chunks = t_dim // RWKV_CHUNK
    tile = pl.BlockSpec((RWKV_CHUNK, pairs * LANES), lambda c, p: (c, p))
    op = pl.BlockSpec((1, pairs, LANES, LANES), lambda c, p: (c, p, 0, 0))
    op_shape = jax.ShapeDtypeStruct((n_chunks, RWKV_PAIRS, LANES, LANES), F32)
    row_shape = jax.ShapeDtypeStruct((t_dim, RWKV_WIDTH), F32)
    return pl.pallas_call(
        functools.partial(_rwkv_chunk_kernel, pairs=pairs),
        grid=(n_chunks, RWKV_PAIRS // pairs),
        in_specs=[tile] * 6,
        out_specs=[op, op, tile, tile],
        out_shape=[op_shape, op_shape, row_shape, row_shape],
        compiler_params=_cparams(("arbitrary", "arbitrary")),
        name="rwkv_chunk",
    )(r, k, v, lw, an, bn)


def _rwkv_sweep_kernel(m_ref, n_ref, q_ref, y0_ref, y_o, z_ref, *, pairs, chunks):
    @pl.when(pl.program_id(1) == 0)
    def _():
        z_ref[...] = jnp.zeros_like(z_ref)

    sl = [slice(p * LANES, (p + 1) * LANES) for p in range(pairs)]
    z = [z_ref[p] for p in range(pairs)]
    for c in range(chunks):
        rows = slice(c * RWKV_CHUNK, (c + 1) * RWKV_CHUNK)
        zs = [_split(zp) for zp in z]
        ys = [_dot3s(q_ref[rows, sl[p]], zs[p]) for p in range(pairs)]
        zn = [_dot3s(m_ref[c, p], zs[p]) for p in range(pairs)]
        for p in range(pairs):
            y_o[rows, sl[p]] = ys[p] + y0_ref[rows, sl[p]]
        z = [zn[p] + n_ref[c, p] for p in range(pairs)]
    for p in range(pairs):
        z_ref[p] = z[p]


def _rwkv_sweep(m_op, n_op, q_op, y0, batch, seq):
    t_dim = q_op.shape[0]
    pairs = RWKV_PAIRS
    chunks = min(4, seq // RWKV_CHUNK)
    steps = seq // (RWKV_CHUNK * chunks)
    tile = pl.BlockSpec((RWKV_CHUNK * chunks, pairs * LANES), lambda b, s: (b * steps + s, 0))
    op = pl.BlockSpec((chunks, pairs, LANES, LANES), lambda b, s: (b * steps + s, 0, 0, 0))
    return pl.pallas_call(
        functools.partial(_rwkv_sweep_kernel, pairs=pairs, chunks=chunks),
        grid=(batch, steps),
        in_specs=[op, op, tile, tile],
        out_specs=tile,
        out_shape=jax.ShapeDtypeStruct((t_dim, RWKV_WIDTH), F32),
        scratch_shapes=[pltpu.VMEM((pairs, LANES, LANES), F32)],
        compiler_params=_cparams(("arbitrary", "arbitrary")),
        name="rwkv_sweep",
    )(m_op, n_op, q_op, y0)


def _rwkv_out_kernel(y_ref, r_ref, k_ref, v_ref, g_ref, rk_ref, lg_ref, lb_ref, bd_ref, o_ref):
    bd = bd_ref[...]
    y = y_ref[...]
    mu = _head_sum(y, bd) * (1.0 / RWKV_HEAD_DIM)
    yc = y - mu
    var = _head_sum(yc * yc, bd) * (1.0 / RWKV_HEAD_DIM)
    yn = yc * lax.rsqrt(var + RWKV_GN_EPS) * lg_ref[...] + lb_ref[...]
    bonus = _head_sum(r_ref[...] * k_ref[...] * rk_ref[...], bd) * v_ref[...]
    o_ref[...] = ((yn + bonus) * g_ref[...]).astype(o_ref.dtype)


def _rwkv_out(y, r, k, v, g, r_k, ln_g, ln_b, bd):
    t_dim = y.shape[0]
    ts = min(256, t_dim)
    tile = pl.BlockSpec((ts, RWKV_WIDTH), lambda i: (i, 0))
    vec = pl.BlockSpec((1, RWKV_WIDTH), lambda i: (0, 0))
    return pl.pallas_call(
        _rwkv_out_kernel, grid=(t_dim // ts,),
        in_specs=[tile] * 5 + [vec] * 3 + [pl.BlockSpec((LANES, LANES), lambda i: (0, 0))],
        out_specs=tile,
        out_shape=jax.ShapeDtypeStruct((t_dim, RWKV_WIDTH), BF16),
        compiler_params=_cparams(("arbitrary",)),
        name="rwkv_out",
    )(y, r, k, v, g, r_k, ln_g, ln_b, bd)


def _merge_kernel(yp_ref, ya_ref, yr_ref, wp_ref, wa_ref, wr_ref, g0_ref, g1_ref, g2_ref,
                  b0_ref, b1_ref, b2_ref, o_ref, wpb, wab, wrb):
    @pl.when(pl.program_id(1) == 0)
    def _():
        wpb[...] = wp_ref[...].astype(BF16)
        wab[...] = wa_ref[...].astype(BF16)
        wrb[...] = wr_ref[...].astype(BF16)

    m = jax.nn.sigmoid(g0_ref[...] + b0_ref[...]) * _dot(yp_ref[...], wpb[...])
    m = m + jax.nn.sigmoid(g1_ref[...] + b1_ref[...]) * _dot(ya_ref[...], wab[...])
    m = m + jax.nn.sigmoid(g2_ref[...] + b2_ref[...]) * _dot(yr_ref[...], wrb[...])
    o_ref[...] = m.astype(o_ref.dtype)


def _merge(y_pool, y_attn, y_rwkv, w_pool, w_attn, w_rwkv, layer, gate_logits, b_gate):
    t_dim = y_pool.shape[0]
    tm = min(1024, t_dim)
    tn = 512
    nb = D_MODEL // tn
    act = lambda width: pl.BlockSpec((tm, width), lambda j, i: (i, 0))
    wgt = lambda width: pl.BlockSpec((None, width, tn), lambda j, i: (layer, 0, j))
    gate = lambda br: pl.BlockSpec((tm, tn), lambda j, i: (i, br * nb + j))
    bias = lambda br: pl.BlockSpec((1, tn), lambda j, i: (0, br * nb + j))
    return pl.pallas_call(
        _merge_kernel, grid=(nb, t_dim // tm),
        in_specs=[act(POOL_WIDTH), act(SB_WIDTH), act(RWKV_WIDTH),
                  wgt(POOL_WIDTH), wgt(SB_WIDTH), wgt(RWKV_WIDTH),
                  gate(0), gate(1), gate(2), bias(0), bias(1), bias(2)],
        out_specs=pl.BlockSpec((tm, tn), lambda j, i: (i, j)),
        out_shape=jax.ShapeDtypeStruct((t_dim, D_MODEL), BF16),
        scratch_shapes=[pltpu.VMEM((POOL_WIDTH, tn), BF16), pltpu.VMEM((SB_WIDTH, tn), BF16),
                        pltpu.VMEM((RWKV_WIDTH, tn), BF16)],
        compiler_params=_cparams(("arbitrary", "arbitrary"), VMEM_BUDGET),
        name="merge",
    )(y_pool, y_attn, y_rwkv, w_pool, w_attn, w_rwkv, gate_logits, gate_logits, gate_logits,
      b_gate, b_gate, b_gate)


def _ffn_up_kernel(a_ref, wa_ref, wl_ref, cw_ref, cb_ref, o_ref, wb_ref, halo_ref, *, tiles_per_seq):
    i = pl.program_id(1)
    tm, tn = o_ref.shape

    @pl.when(i == 0)
    def _():
        _cast_panel(wa_ref, wb_ref.at[:, :tn])
        _cast_panel(wl_ref, wb_ref.at[:, tn:])
        halo_ref[...] = jnp.zeros_like(halo_ref)

    prev = jnp.where(i % tiles_per_seq == 0, 0.0, halo_ref[...])
    row = lax.broadcasted_iota(jnp.int32, (DOT_ROWS, 1), 0)
    for c in range(tm // DOT_ROWS):
        rows = slice(c * DOT_ROWS, (c + 1) * DOT_ROWS)
        res = _dot(a_ref[rows, :], wb_ref[...])
        act, lin = res[:, :tn], res[:, tn:]
        p1 = jnp.where(row == 0, prev[7:8, :], pltpu.roll(act, 1, axis=0))
        p2 = jnp.where(row == 0, prev[6:7, :],
                       jnp.where(row == 1, prev[7:8, :], pltpu.roll(act, 2, axis=0)))
        cv = p2 * cw_ref[0:1, :] + p1 * cw_ref[1:2, :] + act * cw_ref[2:3, :] + cb_ref[...]
        gelu = 0.5 * cv * (1.0 + lax.erf(cv * (2.0 ** -0.5)))
        o_ref[rows, :] = (gelu * lin).astype(o_ref.dtype)
        prev = act[DOT_ROWS - 8:, :]
    halo_ref[...] = prev


def _ffn_up(a, w_up, layer, conv_w, conv_b, seq):
    t_dim, k_dim = a.shape
    tm = min(2048, seq)
    tn = 256
    nb = D_FF // tn
    return pl.pallas_call(
        functools.partial(_ffn_up_kernel, tiles_per_seq=seq // tm),
        grid=(nb, t_dim // tm),
        in_specs=[pl.BlockSpec((tm, k_dim), lambda j, i: (i, 0)),
                  pl.BlockSpec((None, k_dim, tn), lambda j, i: (layer, 0, j)),
                  pl.BlockSpec((None, k_dim, tn), lambda j, i: (layer, 0, nb + j)),
                  pl.BlockSpec((3, tn), lambda j, i: (0, j)),
                  pl.BlockSpec((1, tn), lambda j, i: (0, j))],
        out_specs=pl.BlockSpec((tm, tn), lambda j, i: (i, j)),
        out_shape=jax.ShapeDtypeStruct((t_dim, D_FF), BF16),
        scratch_shapes=[pltpu.VMEM((k_dim, 2 * tn), BF16), pltpu.VMEM((8, tn), F32)],
        compiler_params=_cparams(("arbitrary", "arbitrary"), VMEM_BUDGET),
        name="ffn_up",
    )(a, w_up, w_up, conv_w, conv_b.reshape(1, D_FF))


def kernel(x, w_in, b_gate, pool_w, pool_scale, rwkv_mu, rwkv_w0, rwkv_w2, rwkv_a0, rwkv_a2, rwkv_g2, rwkv_k_k, rwkv_k_a, rwkv_r_k, rwkv_ln_g, rwkv_ln_b, w_branch_pool, w_branch_attn, w_branch_rwkv, w_out, ln1_g, ln1_b, w_up, ffn_conv_w, ffn_conv_b, w_down, ln2_g, ln2_b):
    batch, seq, _ = x.shape
    t_dim = batch * seq
    xf = x.reshape(t_dim, D_MODEL)
    xb = xf.astype(BF16)
    half = lax.broadcasted_iota(jnp.int32, (LANES, LANES), 0) // RWKV_HEAD_DIM
    bd = (half == half.T).astype(BF16)
    vec = lambda p: p.reshape(1, -1)
    col = lax.broadcasted_iota(jnp.int32, (1, N_MAIN), 1)
    q_cols = (col >= POOL_WIDTH) & (col < POOL_WIDTH + SB_WIDTH)
    main_scale = jnp.where(q_cols, SB_HEAD_DIM ** -0.5 * math.log2(math.e), 1.0).astype(F32)
    w_in_t = jnp.swapaxes(w_in, 1, 2)
    for l in range(DEPTH):
        main = _mm_t(xb, w_in_t, l, row0=0, n_cols=N_MAIN, out_dtype=BF16, col_scale=main_scale)
        urw = _mm_t(xb, w_in_t, l, row0=N_MAIN, n_cols=RWKV_PAD, out_dtype=F32)
        gate_logits = _mm_t(xb, w_in_t, l, row0=GATE_OFF, n_cols=3 * D_MODEL, out_dtype=BF16)

        y_pool = _pool(main, pool_w[l], pool_scale[l], seq)
        y_attn = _attention(main, batch, seq)

        wc = jnp.zeros((RWKV_LOW, 3 * RWKV_WIDTH), F32)
        wc = wc.at[:RWKV_DECAY_RANK, :RWKV_WIDTH].set(rwkv_w2[l])
        wc = wc.at[RWKV_DECAY_RANK:RWKV_DECAY_RANK + RWKV_A_RANK, RWKV_WIDTH:2 * RWKV_WIDTH].set(rwkv_a2[l])
        lo = RWKV_DECAY_RANK + RWKV_A_RANK
        wc = wc.at[lo:lo + RWKV_GATE_RANK, 2 * RWKV_WIDTH:].set(rwkv_g2[l])
        mu = jnp.pad(rwkv_mu[l], (0, RWKV_PAD - RWKV_SHIFT_WIDTH)).reshape(1, RWKV_PAD)
        r, k, v, lw, an, bn, g = _rwkv_prep(urw, seq, mu, vec(rwkv_w0[l]), vec(rwkv_a0[l]),
                                            vec(rwkv_k_k[l]), vec(rwkv_k_a[l]), wc, bd)
        m_op, n_op, q_op, y0 = _rwkv_chunks(r, k, v, lw, an, bn)
        y_scan = _rwkv_sweep(m_op, n_op, q_op, y0, batch, seq)
        y_rwkv = _rwkv_out(y_scan, r, k, v, g, vec(rwkv_r_k[l]), vec(rwkv_ln_g[l]),
                           vec(rwkv_ln_b[l]), bd)

        merged = _merge(y_pool, y_attn, y_rwkv, w_branch_pool, w_branch_attn, w_branch_rwkv, l,
                        gate_logits, vec(b_gate[l]))
        sub = _mm(merged, w_out, l, out_dtype=BF16, tm=1024, tn=512)
        xf, xb = _ln(xf, sub, ln1_g[l], ln1_b[l])

        h = _ffn_up(xb, w_up, l, ffn_conv_w[l], ffn_conv_b[l], seq)
        ffn = _mm(h, w_down, l, out_dtype=BF16, tm=256, tn=512, single_buffer_w=True)
        xf, xb = _ln(xf, ffn, ln2_g[l], ln2_b[l])
    return xf.reshape(batch, seq, D_MODEL)
```

```python
import functools
import math

import jax
import jax.numpy as jnp
from jax import lax
from jax.experimental import pallas as pl
from jax.experimental.pallas import tpu as pltpu

F32 = jnp.float32
BF16 = jnp.bfloat16

D_MODEL = 4096
DEPTH = 2
POOL_WIDTH = 1024
POOL_WINDOWS = (2, 4, 8, 16)
POOL_GROUP_DIM = 256
POOL_HALO = 16
SB_HEAD_DIM = 128
SB_WIDTH = 2048
SB_HEADS = 16
RWKV_WIDTH = 1024
RWKV_HEAD_DIM = 64
RWKV_DECAY_RANK = 64
RWKV_A_RANK = 64
RWKV_GATE_RANK = 160
RWKV_SHIFT_WIDTH = 3 * RWKV_WIDTH + RWKV_DECAY_RANK + RWKV_A_RANK + RWKV_GATE_RANK
RWKV_GN_EPS = 64e-5
N_MAIN = POOL_WIDTH + 3 * SB_WIDTH
RWKV_PAD = 3584
RWKV_LOW = 384
GATE_OFF = N_MAIN + RWKV_SHIFT_WIDTH
D_FF = 11008
DEEPNORM_ALPHA = (2 * DEPTH) ** 0.25
LN_EPS = 1e-5

LANES = 128
RWKV_CHUNK = 64
RWKV_PAIRS = RWKV_WIDTH // LANES
VMEM_BUDGET = 58 * 1024 * 1024
CAST_ROWS = 256
DOT_ROWS = 256
DOT_ROWS_T = 512


def _cparams(sem, vmem=None):
    return pltpu.CompilerParams(dimension_semantics=sem, vmem_limit_bytes=vmem)


def _dot(a, b, dims=None, prec=None):
    if dims is None:
        dims = (((a.ndim - 1,), (0,)), ((), ()))
    return lax.dot_general(a, b, dims, precision=prec, preferred_element_type=F32)


def _dot32(a, b, dims=None):
    return _dot(a, b, dims, prec=lax.Precision.HIGHEST)


def _split(a):
    hi = a.astype(BF16)
    return hi, (a - hi.astype(F32)).astype(BF16)


def _dot3s(a, b_split):
    ah, al = _split(a)
    bh, bl = b_split
    return _dot(ah, bh) + _dot(ah, bl) + _dot(al, bh)


NT = (((1,), (1,)), ((), ()))
TN = (((0,), (0,)), ((), ()))


def _cast_panel(w_ref, wb_ref):
    def cast(c, carry):
        r = pl.multiple_of(c * CAST_ROWS, CAST_ROWS)
        wb_ref[pl.ds(r, CAST_ROWS), :] = w_ref[pl.ds(r, CAST_ROWS), :].astype(BF16)
        return carry
    lax.fori_loop(0, wb_ref.shape[0] // CAST_ROWS, cast, 0)


def _mm_kernel(a_ref, w_ref, o_ref, wb_ref):
    tm = o_ref.shape[0]

    @pl.when(pl.program_id(1) == 0)
    def _():
        _cast_panel(w_ref, wb_ref)

    for c in range(tm // DOT_ROWS):
        rows = slice(c * DOT_ROWS, (c + 1) * DOT_ROWS)
        o_ref[rows, :] = _dot(a_ref[rows, :], wb_ref[...]).astype(o_ref.dtype)


def _mm(a, w, layer, *, out_dtype, tm, tn, single_buffer_w=False):
    m_dim, k_dim = a.shape
    n_cols = w.shape[2]
    tm = min(tm, m_dim)
    w_mode = dict(pipeline_mode=pl.Buffered(1)) if single_buffer_w else {}
    return pl.pallas_call(
        _mm_kernel,
        grid=(n_cols // tn, m_dim // tm),
        in_specs=[pl.BlockSpec((tm, k_dim), lambda j, i: (i, 0)),
                  pl.BlockSpec((None, k_dim, tn), lambda j, i: (layer, 0, j), **w_mode)],
        out_specs=pl.BlockSpec((tm, tn), lambda j, i: (i, j)),
        out_shape=jax.ShapeDtypeStruct((m_dim, n_cols), out_dtype),
        scratch_shapes=[pltpu.VMEM((k_dim, tn), BF16)],
        compiler_params=_cparams(("arbitrary", "arbitrary"), VMEM_BUDGET),
        name="mm",
    )(a, w)


def _mm_t_kernel(*refs, scaled):
    a_ref, w_ref = refs[0], refs[1]
    s_ref = refs[2] if scaled else None
    o_ref, wb_ref = refs[-2], refs[-1]
    tm, tn = o_ref.shape

    @pl.when(pl.program_id(1) == 0)
    def _():
        _cast_panel(w_ref, wb_ref)

    for c in range(tm // DOT_ROWS_T):
        rows = slice(c * DOT_ROWS_T, (c + 1) * DOT_ROWS_T)
        res = _dot(a_ref[rows, :], wb_ref[...], NT)
        if scaled:
            res = res * s_ref[...]
        o_ref[rows, :] = res.astype(o_ref.dtype)


def _mm_t(a, wt, layer, *, row0, n_cols, out_dtype, tm=1024, tn=512, col_scale=None):
    m_dim, k_dim = a.shape
    tm = min(tm, m_dim)
    operands = [a, wt]
    in_specs = [pl.BlockSpec((tm, k_dim), lambda j, i: (i, 0)),
                pl.BlockSpec((None, pl.Element(tn), pl.Element(k_dim)),
                             lambda j, i: (layer, pl.multiple_of(row0 + j * tn, 32), 0))]
    if col_scale is not None:
        operands.append(col_scale)
        in_specs.append(pl.BlockSpec((1, tn), lambda j, i: (0, j)))
    return pl.pallas_call(
        functools.partial(_mm_t_kernel, scaled=col_scale is not None),
        grid=(n_cols // tn, m_dim // tm),
        in_specs=in_specs,
        out_specs=pl.BlockSpec((tm, tn), lambda j, i: (i, j)),
        out_shape=jax.ShapeDtypeStruct((m_dim, n_cols), out_dtype),
        scratch_shapes=[pltpu.VMEM((tn, k_dim), BF16)],
        compiler_params=_cparams(("arbitrary", "arbitrary"), VMEM_BUDGET),
        name="mm_t",
    )(*operands)


def _ln_kernel(x_ref, y_ref, g_ref, b_ref, of_ref, ob_ref):
    z = DEEPNORM_ALPHA * x_ref[...] + y_ref[...]
    mu = jnp.mean(z, axis=-1, keepdims=True)
    zc = z - mu
    var = jnp.mean(zc * zc, axis=-1, keepdims=True)
    out = zc * lax.rsqrt(var + LN_EPS) * g_ref[...] + b_ref[...]
    of_ref[...] = out
    ob_ref[...] = out.astype(BF16)


def _ln(x, y, g, b):
    t_dim = x.shape[0]
    tm = min(128, t_dim)
    row = pl.BlockSpec((tm, D_MODEL), lambda i: (i, 0))
    vec = pl.BlockSpec((1, D_MODEL), lambda i: (0, 0))
    return pl.pallas_call(
        _ln_kernel, grid=(t_dim // tm,),
        in_specs=[row, row, vec, vec], out_specs=[row, row],
        out_shape=[jax.ShapeDtypeStruct((t_dim, D_MODEL), F32),
                   jax.ShapeDtypeStruct((t_dim, D_MODEL), BF16)],
        compiler_params=_cparams(("arbitrary",)),
        name="ln",
    )(x, y, g.reshape(1, D_MODEL), b.reshape(1, D_MODEL))


def _pool_kernel(u_ref, h_ref, pw_ref, ps_ref, o_ref, *, ts, tiles_per_seq):
    it = pl.program_id(0) % tiles_per_seq
    x = u_ref[...].astype(F32)
    halo = jnp.where(it == 0, 0.0, h_ref[...].astype(F32))
    xe = jnp.concatenate([halo, x], axis=0)
    t_seq = lax.broadcasted_iota(jnp.int32, (ts, 1), 0) + it * ts
    for g, win in enumerate(POOL_WINDOWS):
        cols = slice(g * POOL_GROUP_DIM, (g + 1) * POOL_GROUP_DIM)
        s = xe[:, cols]
        span = 1
        while span < win:
            s = s + pltpu.roll(s, span, axis=0)
            span *= 2
        cnt = jnp.minimum(t_seq + 1, win).astype(F32)
        d = s[POOL_HALO:, :] / cnt - x[:, cols]
        y = _dot(d.astype(BF16), pw_ref[g].astype(BF16))
        o_ref[:, cols] = (y * ps_ref[:, cols]).astype(o_ref.dtype)


def _pool(main, pool_w, pool_scale, seq):
    t_dim = main.shape[0]
    ts = min(512, seq)
    hb = ts // POOL_HALO
    return pl.pallas_call(
        functools.partial(_pool_kernel, ts=ts, tiles_per_seq=seq // ts),
        grid=(t_dim // ts,),
        in_specs=[pl.BlockSpec((ts, POOL_WIDTH), lambda i: (i, 0)),
                  pl.BlockSpec((POOL_HALO, POOL_WIDTH), lambda i: (jnp.maximum(i * hb - 1, 0), 0)),
                  pl.BlockSpec((4, POOL_GROUP_DIM, POOL_GROUP_DIM), lambda i: (0, 0, 0)),
                  pl.BlockSpec((1, POOL_WIDTH), lambda i: (0, 0))],
        out_specs=pl.BlockSpec((ts, POOL_WIDTH), lambda i: (i, 0)),
        out_shape=jax.ShapeDtypeStruct((t_dim, POOL_WIDTH), BF16),
        compiler_params=_cparams(("arbitrary",)),
        name="pool",
    )(main, main, pool_w, pool_scale.reshape(1, POOL_WIDTH))


def _attn_kernel(q_ref, k_ref, v_ref, r_ref, o_ref, *, tq, tk):
    iq = pl.program_id(2)
    rmat = r_ref[...]

    def blocks(q, row0, js, acc, run, masked):
        n = q.shape[0]
        starts = [pl.multiple_of(j * tk, tk) for j in js]
        z = [_dot(q, k_ref[pl.ds(r0, tk), :], NT) for r0 in starts]
        nz = [-zj for zj in z]
        lk = [jnp.minimum(m, 0.0) - jnp.log2(1.0 + jnp.exp2(jnp.minimum(zj, m)))
              for zj, m in zip(z, nz)]
        if masked:
            row = lax.broadcasted_iota(jnp.int32, (n, tk), 0) + row0
            causal = [lax.broadcasted_iota(jnp.int32, (n, tk), 1) + j * tk < row for j in js]
            lk = [jnp.where(c, x, 0.0) for c, x in zip(causal, lk)]
        later = [_dot(x.astype(BF16), rmat) for x in lk]
        base = [zj + lt for zj, lt in zip(z, later)]
        for b, r0 in enumerate(starts):
            att = jnp.exp2(base[b] + run)
            if masked:
                att = jnp.where(causal[b], att, 0.0)
            acc = acc + _dot(att.astype(BF16), v_ref[pl.ds(r0, tk), :])
            run = run + jnp.sum(lk[b], axis=1, keepdims=True)
        return acc, run

    def block(q, row0, j, acc, run, masked):
        return blocks(q, row0, [j], acc, run, masked)

    zero_acc = jnp.zeros((tk, SB_HEAD_DIM), F32)
    zero_run = jnp.zeros((tk, 1), F32)
    q_top, q_bot = q_ref[:tk, :], q_ref[tk:, :]
    row_top = iq * tq
    acc_t, run_t = block(q_top, row_top, 2 * iq, zero_acc, zero_run, True)
    acc_b, run_b = block(q_bot, row_top + tk, 2 * iq + 1, zero_acc, zero_run, True)
    acc_b, run_b = block(q_bot, row_top + tk, 2 * iq, acc_b, run_b, False)
    acc = jnp.concatenate([acc_t, acc_b], axis=0)
    run = jnp.concatenate([run_t, run_b], axis=0)
    q = q_ref[...]

    def pair(n, carry):
        j = 2 * iq - 1 - 2 * n
        return blocks(q, 0, [j, j - 1], *carry, False)

    acc, run = lax.fori_loop(0, iq, pair, (acc, run))
    o_ref[...] = acc.astype(o_ref.dtype)


def _attention(main, batch, seq):
    t_dim = main.shape[0]
    tq = min(512, seq)
    tk = tq // 2
    nq = seq // tq
    q_blk0 = POOL_WIDTH // LANES
    k_blk0 = q_blk0 + SB_HEADS
    v_blk0 = k_blk0 + SB_HEADS
    rmat = (lax.broadcasted_iota(jnp.int32, (tk, tk), 0)
            >= lax.broadcasted_iota(jnp.int32, (tk, tk), 1)).astype(BF16)
    return pl.pallas_call(
        functools.partial(_attn_kernel, tq=tq, tk=tk),
        grid=(batch, SB_HEADS, nq),
        in_specs=[pl.BlockSpec((tq, SB_HEAD_DIM), lambda b, h, i: (b * nq + i, q_blk0 + h)),
                  pl.BlockSpec((seq, SB_HEAD_DIM), lambda b, h, i: (b, k_blk0 + h)),
                  pl.BlockSpec((seq, SB_HEAD_DIM), lambda b, h, i: (b, v_blk0 + h)),
                  pl.BlockSpec((tk, tk), lambda b, h, i: (0, 0))],
        out_specs=pl.BlockSpec((tq, SB_HEAD_DIM), lambda b, h, i: (b * nq + i, h)),
        out_shape=jax.ShapeDtypeStruct((t_dim, SB_WIDTH), BF16),
        compiler_params=_cparams(("arbitrary", "arbitrary", "arbitrary")),
        name="attn",
    )(main, main, main, rmat)


def _head_sum(x, bd):
    hi, lo = _split(x)
    return jnp.concatenate(
        [_dot(hi[:, c * LANES:(c + 1) * LANES], bd) + _dot(lo[:, c * LANES:(c + 1) * LANES], bd)
         for c in range(x.shape[1] // LANES)], axis=1)


def _rwkv_prep_kernel(u_ref, h_ref, mu_ref, w0_ref, a0_ref, kk_ref, ka_ref, wch_ref, wcl_ref, bd_ref,
                      r_o, k_o, v_o, lw_o, an_o, bn_o, g_o, *, ts, tiles_per_seq):
    it = pl.program_id(0) % tiles_per_seq
    u = u_ref[...]
    last = jnp.where(it == 0, 0.0, h_ref[7:8, :])
    row = lax.broadcasted_iota(jnp.int32, (ts, 1), 0)
    prev = jnp.where(row == 0, last, pltpu.roll(u, 1, axis=0))
    x = u + (prev - u) * mu_ref[...]
    w_dim = RWKV_WIDTH
    r, k, v = x[:, :w_dim], x[:, w_dim:2 * w_dim], x[:, 2 * w_dim:3 * w_dim]
    low = x[:, 3 * w_dim:3 * w_dim + RWKV_LOW]
    lane = lax.broadcasted_iota(jnp.int32, (1, RWKV_LOW), 1)
    low = jnp.where(lane < RWKV_DECAY_RANK, jnp.tanh(low),
                    jnp.where(lane < RWKV_DECAY_RANK + RWKV_A_RANK, low, jax.nn.sigmoid(low)))
    mm = _dot3s(low, (wch_ref[...], wcl_ref[...]))
    w_lin = w0_ref[...] + mm[:, :w_dim]
    soft = jnp.maximum(-w_lin, 0.0) + jnp.log(1.0 + jnp.exp(-jnp.abs(w_lin)))
    log_decay = -jnp.exp(-soft - 0.5)
    a = jax.nn.sigmoid(a0_ref[...] + mm[:, w_dim:2 * w_dim])
    g = mm[:, 2 * w_dim:]
    kk = k * kk_ref[...]
    norm = jnp.sqrt(_head_sum(kk * kk, bd_ref[...]))
    kk = kk / jnp.maximum(norm, 1e-12)
    r_o[...] = r
    k_o[...] = k * (1.0 + (a - 1.0) * ka_ref[...])
    v_o[...] = v
    lw_o[...] = log_decay
    an_o[...] = -kk
    bn_o[...] = kk * a
    g_o[...] = g


def _rwkv_prep(urw, seq, mu, w0, a0, k_k, k_a, wc, bd):
    t_dim = urw.shape[0]
    wc_hi, wc_lo = _split(wc)
    ts = min(256, seq)
    hb = ts // 8
    wide = pl.BlockSpec((ts, RWKV_PAD), lambda i: (i, 0))
    halo = pl.BlockSpec((8, RWKV_PAD), lambda i: (jnp.maximum(i * hb - 1, 0), 0))
    vec = pl.BlockSpec((1, RWKV_WIDTH), lambda i: (0, 0))
    out = pl.BlockSpec((ts, RWKV_WIDTH), lambda i: (i, 0))
    return pl.pallas_call(
        functools.partial(_rwkv_prep_kernel, ts=ts, tiles_per_seq=seq // ts),
        grid=(t_dim // ts,),
        in_specs=[wide, halo, pl.BlockSpec((1, RWKV_PAD), lambda i: (0, 0)), vec, vec, vec, vec,
                  pl.BlockSpec((RWKV_LOW, 3 * RWKV_WIDTH), lambda i: (0, 0)),
                  pl.BlockSpec((RWKV_LOW, 3 * RWKV_WIDTH), lambda i: (0, 0)),
                  pl.BlockSpec((LANES, LANES), lambda i: (0, 0))],
        out_specs=[out] * 7,
        out_shape=[jax.ShapeDtypeStruct((t_dim, RWKV_WIDTH), F32)] * 7,
        compiler_params=_cparams(("arbitrary",), VMEM_BUDGET),
        name="rwkv_prep",
    )(urw, urw, mu, w0, a0, k_k, k_a, wc_hi, wc_lo, bd)


def _rwkv_chunk_kernel(r_ref, k_ref, v_ref, lw_ref, an_ref, bn_ref,
                       m_o, n_o, q_o, y_o, *, pairs):
    c_len = RWKV_CHUNK
    n2 = 2 * c_len
    ri = lax.broadcasted_iota(jnp.int32, (n2, n2), 0)
    ci = lax.broadcasted_iota(jnp.int32, (n2, n2), 1)
    strict = (ri % c_len) > (ci % c_len)
    incl = (ri % c_len) >= (ci % c_len)
    eye = (ri == ci).astype(F32)
    tri = (lax.broadcasted_iota(jnp.int32, (c_len, c_len), 0)
           >= lax.broadcasted_iota(jnp.int32, (c_len, c_len), 1)).astype(F32)
    lane = lax.broadcasted_iota(jnp.int32, (n2, LANES), 1)
    rowh = lax.broadcasted_iota(jnp.int32, (n2, LANES), 0)
    own = (lane < RWKV_HEAD_DIM) == (rowh < c_len)

    def stack(x):
        return jnp.where(own, jnp.concatenate([x, x], axis=0), 0.0)

    cum_all = _dot32(tri, lw_ref[...])
    every = range(pairs)
    sl = [slice(p * LANES, (p + 1) * LANES) for p in every]
    cum = [cum_all[:, s] for s in sl]
    tot = [c[c_len - 1:c_len, :] for c in cum]
    r_t = [stack(r_ref[:, sl[p]] * jnp.exp(cum[p])) for p in every]
    a_t = [stack(an_ref[:, sl[p]] * jnp.exp(cum[p] - lw_ref[:, sl[p]])).astype(BF16) for p in every]
    b_t = [stack(bn_ref[:, sl[p]] * jnp.exp(-cum[p])).astype(BF16) for p in every]
    k_t = [stack(k_ref[:, sl[p]] * jnp.exp(-cum[p])).astype(BF16) for p in every]
    b_e = [stack(bn_ref[:, sl[p]] * jnp.exp(tot[p] - cum[p])).astype(BF16) for p in every]
    k_e = [stack(k_ref[:, sl[p]] * jnp.exp(tot[p] - cum[p])).astype(BF16) for p in every]
    v_s = [stack(v_ref[:, sl[p]]).astype(BF16) for p in every]
    gram = [_dot(jnp.concatenate([a_t[p], r_t[p].astype(BF16)], axis=0),
                 jnp.concatenate([b_t[p], k_t[p]], axis=0), NT) for p in every]
    a_ab = [jnp.where(strict, g[:n2, :n2], 0.0) for g in gram]
    a_ak = [jnp.where(strict, g[:n2, n2:], 0.0).astype(BF16) for g in gram]
    r_ab = [jnp.where(incl, g[n2:, :n2], 0.0).astype(BF16) for g in gram]
    r_ak = [jnp.where(incl, g[n2:, n2:], 0.0).astype(BF16) for g in gram]
    inv = [eye + a for a in a_ab]
    pw = [a.astype(BF16) for a in a_ab]
    pw = [_dot(a, a) for a in pw]
    akv = [_dot(a_ak[p], v_s[p]).astype(BF16) for p in every]
    span = 2
    while 2 * span < c_len:
        pwb = [a.astype(BF16) for a in pw]
        both = [_dot(pwb[p], jnp.concatenate([pwb[p], inv[p].astype(BF16)], axis=1)) for p in every]
        pw = [b[:, :n2] for b in both]
        inv = [inv[p] + both[p][:, n2:] for p in every]
        span *= 2
    inv = [inv[p] + _dot(pw[p].astype(BF16), inv[p].astype(BF16)) for p in every]
    wx = [_dot(inv[p].astype(BF16), jnp.concatenate([a_t[p], akv[p]], axis=1)).astype(BF16)
          for p in every]
    qy = [_dot(r_ab[p], wx[p]) for p in every]
    rv = [_dot(r_ak[p], v_s[p]) for p in every]
    mn = [_dot(b_e[p], wx[p], TN) for p in every]
    kv = [_dot(k_e[p], v_s[p], TN) for p in every]
    for p in every:
        q_s = r_t[p] + qy[p][:, :n2]
        y_s = qy[p][:, n2:] + rv[p]
        m_o[0, p] = eye * jnp.exp(tot[p]) + mn[p][:, :n2]
        n_o[0, p] = mn[p][:, n2:] + kv[p]
        q_o[:, sl[p]] = q_s[:c_len] + q_s[c_len:]
        y_o[:, sl[p]] = y_s[:c_len] + y_s[c_len:]


def _rwkv_chunks(r, k, v, lw, an, bn):
    t_dim = r.shape[0]
    pairs = RWKV_PAIRS
    n_chunks = t_dim // RWKV_CHUNK
    tile = pl.BlockSpec((RWKV_CHUNK, pairs * LANES), lambda c, p: (c, p))
    op = pl.BlockSpec((1, pairs, LANES, LANES), lambda c, p: (c, p, 0, 0))
    op_shape = jax.ShapeDtypeStruct((n_chunks, RWKV_PAIRS, LANES, LANES), F32)
    row_shape = jax.ShapeDtypeStruct((t_dim, RWKV_WIDTH), F32)
    return pl.pallas_call(
        functools.partial(_rwkv_chunk_kernel, pairs=pairs),
        grid=(n_chunks, RWKV_PAIRS // pairs),
        in_specs=[tile] * 6,
        out_specs=[op, op, tile, tile],
        out_shape=[op_shape, op_shape, row_shape, row_shape],
        compiler_params=_cparams(("arbitrary", "arbitrary")),
        name="rwkv_chunk",
    )(r, k, v, lw, an, bn)


def _rwkv_sweep_kernel(m_ref, n_ref, q_ref, y0_ref, r_ref, k_ref, v_ref, g_ref,
                       rk_ref, lg_ref, lb_ref, bd_ref, o_ref, z_ref, y_ref, *, pairs, chunks):
    @pl.when(pl.program_id(1) == 0)
    def _():
        z_ref[...] = jnp.zeros_like(z_ref)

    sl = [slice(p * LANES, (p + 1) * LANES) for p in range(pairs)]
    z = [z_ref[p] for p in range(pairs)]
    for c in range(chunks):
        rows = slice(c * RWKV_CHUNK, (c + 1) * RWKV_CHUNK)
        zs = [_split(zp) for zp in z]
        ys = [_dot3s(q_ref[rows, sl[p]], zs[p]) for p in range(pairs)]
        zn = [_dot3s(m_ref[c, p], zs[p]) for p in range(pairs)]
        for p in range(pairs):
            y_ref[rows, sl[p]] = ys[p] + y0_ref[rows, sl[p]]
        z = [zn[p] + n_ref[c, p] for p in range(pairs)]
    for p in range(pairs):
        z_ref[p] = z[p]

    bd = bd_ref[...]
    y = y_ref[...]
    mu = _head_sum(y, bd) * (1.0 / RWKV_HEAD_DIM)
    yc = y - mu
    var = _head_sum(yc * yc, bd) * (1.0 / RWKV_HEAD_DIM)
    yn = yc * lax.rsqrt(var + RWKV_GN_EPS) * lg_ref[...] + lb_ref[...]
    bonus = _head_sum(r_ref[...] * k_ref[...] * rk_ref[...], bd) * v_ref[...]
    o_ref[...] = ((yn + bonus) * g_ref[...]).astype(o_ref.dtype)


def _rwkv_sweep(m_op, n_op, q_op, y0, r, k, v, g, r_k, ln_g, ln_b, bd, batch, seq):
    t_dim = q_op.shape[0]
    pairs = RWKV_PAIRS
    chunks = min(4, seq // RWKV_CHUNK)
    steps = seq // (RWKV_CHUNK * chunks)
    rows = RWKV_CHUNK * chunks
    tile = pl.BlockSpec((rows, RWKV_WIDTH), lambda b, s: (b * steps + s, 0))
    op = pl.BlockSpec((chunks, pairs, LANES, LANES), lambda b, s: (b * steps + s, 0, 0, 0))
    vec = pl.BlockSpec((1, RWKV_WIDTH), lambda b, s: (0, 0))
    return pl.pallas_call(
        functools.partial(_rwkv_sweep_kernel, pairs=pairs, chunks=chunks),
        grid=(batch, steps),
        in_specs=[op, op] + [tile] * 6 + [vec] * 3 + [pl.BlockSpec((LANES, LANES), lambda b, s: (0, 0))],
        out_specs=tile,
        out_shape=jax.ShapeDtypeStruct((t_dim, RWKV_WIDTH), BF16),
        scratch_shapes=[pltpu.VMEM((pairs, LANES, LANES), F32), pltpu.VMEM((rows, RWKV_WIDTH), F32)],
        compiler_params=_cparams(("arbitrary", "arbitrary"), VMEM_BUDGET),
        name="rwkv_sweep",
    )(m_op, n_op, q_op, y0, r, k, v, g, r_k, ln_g, ln_b, bd)


def _merge_kernel(yp_ref, ya_ref, yr_ref, wp_ref, wa_ref, wr_ref, g0_ref, g1_ref, g2_ref,
                  b0_ref, b1_ref, b2_ref, o_ref, wpb, wab, wrb):
    @pl.when(pl.program_id(1) == 0)
    def _():
        wpb[...] = wp_ref[...].astype(BF16)
        wab[...] = wa_ref[...].astype(BF16)
        wrb[...] = wr_ref[...].astype(BF16)

    m = jax.nn.sigmoid(g0_ref[...] + b0_ref[...]) * _dot(yp_ref[...], wpb[...])
    m = m + jax.nn.sigmoid(g1_ref[...] + b1_ref[...]) * _dot(ya_ref[...], wab[...])
    m = m + jax.nn.sigmoid(g2_ref[...] + b2_ref[...]) * _dot(yr_ref[...], wrb[...])
    o_ref[...] = m.astype(o_ref.dtype)


def _merge(y_pool, y_attn, y_rwkv, w_pool, w_attn, w_rwkv, layer, gate_logits, b_gate):
    t_dim = y_pool.shape[0]
    tm = min(1024, t_dim)
    tn = 512
    nb = D_MODEL // tn
    act = lambda width: pl.BlockSpec((tm, width), lambda j, i: (i, 0))
    wgt = lambda width: pl.BlockSpec((None, width, tn), lambda j, i: (layer, 0, j))
    gate = lambda br: pl.BlockSpec((tm, tn), lambda j, i: (i, br * nb + j))
    bias = lambda br: pl.BlockSpec((1, tn), lambda j, i: (0, br * nb + j))
    return pl.pallas_call(
        _merge_kernel, grid=(nb, t_dim // tm),
        in_specs=[act(POOL_WIDTH), act(SB_WIDTH), act(RWKV_WIDTH),
                  wgt(POOL_WIDTH), wgt(SB_WIDTH), wgt(RWKV_WIDTH),
                  gate(0), gate(1), gate(2), bias(0), bias(1), bias(2)],
        out_specs=pl.BlockSpec((tm, tn), lambda j, i: (i, j)),
        out_shape=jax.ShapeDtypeStruct((t_dim, D_MODEL), BF16),
        scratch_shapes=[pltpu.VMEM((POOL_WIDTH, tn), BF16), pltpu.VMEM((SB_WIDTH, tn), BF16),
                        pltpu.VMEM((RWKV_WIDTH, tn), BF16)],
        compiler_params=_cparams(("arbitrary", "arbitrary"), VMEM_BUDGET),
        name="merge",
    )(y_pool, y_attn, y_rwkv, w_pool, w_attn, w_rwkv, gate_logits, gate_logits, gate_logits,
      b_gate, b_gate, b_gate)


def _ffn_up_kernel(a_ref, wa_ref, wl_ref, cw_ref, cb_ref, o_ref, wb_ref, halo_ref, *, tiles_per_seq):
    i = pl.program_id(1)
    tm, tn = o_ref.shape

    @pl.when(i == 0)
    def _():
        _cast_panel(wa_ref, wb_ref.at[:, :tn])
        _cast_panel(wl_ref, wb_ref.at[:, tn:])
        halo_ref[...] = jnp.zeros_like(halo_ref)

    prev = jnp.where(i % tiles_per_seq == 0, 0.0, halo_ref[...])
    row = lax.broadcasted_iota(jnp.int32, (DOT_ROWS, 1), 0)
    for c in range(tm // DOT_ROWS):
        rows = slice(c * DOT_ROWS, (c + 1) * DOT_ROWS)
        res = _dot(a_ref[rows, :], wb_ref[...])
        act, lin = res[:, :tn], res[:, tn:]
        p1 = jnp.where(row == 0, prev[7:8, :], pltpu.roll(act, 1, axis=0))
        p2 = jnp.where(row == 0, prev[6:7, :],
                       jnp.where(row == 1, prev[7:8, :], pltpu.roll(act, 2, axis=0)))
        cv = p2 * cw_ref[0:1, :] + p1 * cw_ref[1:2, :] + act * cw_ref[2:3, :] + cb_ref[...]
        gelu = 0.5 * cv * (1.0 + lax.erf(cv * (2.0 ** -0.5)))
        o_ref[rows, :] = (gelu * lin).astype(o_ref.dtype)
        prev = act[DOT_ROWS - 8:, :]
    halo_ref[...] = prev


def _ffn_up(a, w_up, layer, conv_w, conv_b, seq):
    t_dim, k_dim = a.shape
    tm = min(2048, seq)
    tn = 256
    nb = D_FF // tn
    return pl.pallas_call(
        functools.partial(_ffn_up_kernel, tiles_per_seq=seq // tm),
        grid=(nb, t_dim // tm),
        in_specs=[pl.BlockSpec((tm, k_dim), lambda j, i: (i, 0)),
                  pl.BlockSpec((None, k_dim, tn), lambda j, i: (layer, 0, j)),
                  pl.BlockSpec((None, k_dim, tn), lambda j, i: (layer, 0, nb + j)),
                  pl.BlockSpec((3, tn), lambda j, i: (0, j)),
                  pl.BlockSpec((1, tn), lambda j, i: (0, j))],
        out_specs=pl.BlockSpec((tm, tn), lambda j, i: (i, j)),
        out_shape=jax.ShapeDtypeStruct((t_dim, D_FF), BF16),
        scratch_shapes=[pltpu.VMEM((k_dim, 2 * tn), BF16), pltpu.VMEM((8, tn), F32)],
        compiler_params=_cparams(("arbitrary", "arbitrary"), VMEM_BUDGET),
        name="ffn_up",
    )(a, w_up, w_up, conv_w, conv_b.reshape(1, D_FF))


def kernel(x, w_in, b_gate, pool_w, pool_scale, rwkv_mu, rwkv_w0, rwkv_w2, rwkv_a0, rwkv_a2, rwkv_g2, rwkv_k_k, rwkv_k_a, rwkv_r_k, rwkv_ln_g, rwkv_ln_b, w_branch_pool, w_branch_attn, w_branch_rwkv, w_out, ln1_g, ln1_b, w_up, ffn_conv_w, ffn_conv_b, w_down, ln2_g, ln2_b):
    batch, seq, _ = x.shape
    t_dim = batch * seq
    xf = x.reshape(t_dim, D_MODEL)
    xb = xf.astype(BF16)
    half = lax.broadcasted_iota(jnp.int32, (LANES, LANES), 0) // RWKV_HEAD_DIM
    bd = (half == half.T).astype(BF16)
    vec = lambda p: p.reshape(1, -1)
    col = lax.broadcasted_iota(jnp.int32, (1, N_MAIN), 1)
    q_cols = (col >= POOL_WIDTH) & (col < POOL_WIDTH + SB_WIDTH)
    main_scale = jnp.where(q_cols, SB_HEAD_DIM ** -0.5 * math.log2(math.e), 1.0).astype(F32)
    w_in_t = jnp.swapaxes(w_in, 1, 2)
    for l in range(DEPTH):
        main = _mm_t(xb, w_in_t, l, row0=0, n_cols=N_MAIN, out_dtype=BF16, tm=2048,
                     col_scale=main_scale)
        urw = _mm_t(xb, w_in_t, l, row0=N_MAIN, n_cols=RWKV_PAD, out_dtype=F32)
        gate_logits = _mm_t(xb, w_in_t, l, row0=GATE_OFF, n_cols=3 * D_MODEL, out_dtype=BF16,
                            tm=2048)

        y_pool = _pool(main, pool_w[l], pool_scale[l], seq)
        y_attn = _attention(main, batch, seq)

        wc = jnp.zeros((RWKV_LOW, 3 * RWKV_WIDTH), F32)
        wc = wc.at[:RWKV_DECAY_RANK, :RWKV_WIDTH].set(rwkv_w2[l])
        wc = wc.at[RWKV_DECAY_RANK:RWKV_DECAY_RANK + RWKV_A_RANK, RWKV_WIDTH:2 * RWKV_WIDTH].set(rwkv_a2[l])
        lo = RWKV_DECAY_RANK + RWKV_A_RANK
        wc = wc.at[lo:lo + RWKV_GATE_RANK, 2 * RWKV_WIDTH:].set(rwkv_g2[l])
        mu = jnp.pad(rwkv_mu[l], (0, RWKV_PAD - RWKV_SHIFT_WIDTH)).reshape(1, RWKV_PAD)
        r, k, v, lw, an, bn, g = _rwkv_prep(urw, seq, mu, vec(rwkv_w0[l]), vec(rwkv_a0[l]),
                                            vec(rwkv_k_k[l]), vec(rwkv_k_a[l]), wc, bd)
        m_op, n_op, q_op, y0 = _rwkv_chunks(r, k, v, lw, an, bn)
        y_rwkv = _rwkv_sweep(m_op, n_op, q_op, y0, r, k, v, g, vec(rwkv_r_k[l]),
                             vec(rwkv_ln_g[l]), vec(rwkv_ln_b[l]), bd, batch, seq)

        merged = _merge(y_pool, y_attn, y_rwkv, w_branch_pool, w_branch_attn, w_branch_rwkv, l,
                        gate_logits, vec(b_gate[l]))
        sub = _mm(merged, w_out, l, out_dtype=BF16, tm=2048, tn=512)
        xf, xb = _ln(xf, sub, ln1_g[l], ln1_b[l])

        h = _ffn_up(xb, w_up, l, ffn_conv_w[l], ffn_conv_b[l], seq)
        ffn = _mm(h, w_down, l, out_dtype=BF16, tm=512, tn=512, single_buffer_w=True)
        xf, xb = _ln(xf, ffn, ln2_g[l], ln2_b[l])
    return xf.reshape(batch, seq, D_MODEL)
```

```python
import functools
import math

import jax
import jax.numpy as jnp
from jax import lax
from jax.experimental import pallas as pl
from jax.experimental.pallas import tpu as pltpu

F32 = jnp.float32
BF16 = jnp.bfloat16

D_MODEL = 4096
DEPTH = 2
POOL_WIDTH = 1024
POOL_WINDOWS = (2, 4, 8, 16)
POOL_GROUP_DIM = 256
POOL_HALO = 16
RWKV_HALO = 16
SB_HEAD_DIM = 128
SB_WIDTH = 2048
SB_HEADS = 16
RWKV_WIDTH = 1024
RWKV_HEAD_DIM = 64
RWKV_DECAY_RANK = 64
RWKV_A_RANK = 64
RWKV_GATE_RANK = 160
RWKV_SHIFT_WIDTH = 3 * RWKV_WIDTH + RWKV_DECAY_RANK + RWKV_A_RANK + RWKV_GATE_RANK
RWKV_GN_EPS = 64e-5
N_MAIN = POOL_WIDTH + 3 * SB_WIDTH
RWKV_PAD = 3584
RWKV_LOW = 384
GATE_OFF = N_MAIN + RWKV_SHIFT_WIDTH
D_FF = 11008
DEEPNORM_ALPHA = (2 * DEPTH) ** 0.25
LN_EPS = 1e-5

LANES = 128
RWKV_CHUNK = 64
RWKV_PAIRS = RWKV_WIDTH // LANES
VMEM_BUDGET = 58 * 1024 * 1024
CAST_ROWS = 256
DOT_ROWS = 512
DOT_ROWS_T = 512


def _cparams(sem, vmem=None):
    return pltpu.CompilerParams(dimension_semantics=sem, vmem_limit_bytes=vmem)


def _dot(a, b, dims=None, prec=None):
    if dims is None:
        dims = (((a.ndim - 1,), (0,)), ((), ()))
    return lax.dot_general(a, b, dims, precision=prec, preferred_element_type=F32)


def _dot32(a, b, dims=None):
    return _dot(a, b, dims, prec=lax.Precision.HIGHEST)


def _split(a):
    hi = a.astype(BF16)
    return hi, (a - hi.astype(F32)).astype(BF16)


def _dot3s(a, b_split):
    ah, al = _split(a)
    bh, bl = b_split
    return _dot(ah, bh) + _dot(ah, bl) + _dot(al, bh)


NT = (((1,), (1,)), ((), ()))
TN = (((0,), (0,)), ((), ()))


def _cast_panel(w_ref, wb_ref):
    def cast(c, carry):
        r = pl.multiple_of(c * CAST_ROWS, CAST_ROWS)
        wb_ref[pl.ds(r, CAST_ROWS), :] = w_ref[pl.ds(r, CAST_ROWS), :].astype(BF16)
        return carry
    lax.fori_loop(0, wb_ref.shape[0] // CAST_ROWS, cast, 0)


def _mm_kernel(a_ref, w_ref, o_ref, wb_ref):
    tm = o_ref.shape[0]

    @pl.when(pl.program_id(1) == 0)
    def _():
        _cast_panel(w_ref, wb_ref)

    for c in range(tm // DOT_ROWS):
        rows = slice(c * DOT_ROWS, (c + 1) * DOT_ROWS)
        o_ref[rows, :] = _dot(a_ref[rows, :], wb_ref[...]).astype(o_ref.dtype)


def _mm(a, w, layer, *, out_dtype, tm, tn, single_buffer_w=False):
    m_dim, k_dim = a.shape
    n_cols = w.shape[2]
    tm = min(tm, m_dim)
    w_mode = dict(pipeline_mode=pl.Buffered(1)) if single_buffer_w else {}
    return pl.pallas_call(
        _mm_kernel,
        grid=(n_cols // tn, m_dim // tm),
        in_specs=[pl.BlockSpec((tm, k_dim), lambda j, i: (i, 0)),
                  pl.BlockSpec((None, k_dim, tn), lambda j, i: (layer, 0, j), **w_mode)],
        out_specs=pl.BlockSpec((tm, tn), lambda j, i: (i, j)),
        out_shape=jax.ShapeDtypeStruct((m_dim, n_cols), out_dtype),
        scratch_shapes=[pltpu.VMEM((k_dim, tn), BF16)],
        compiler_params=_cparams(("arbitrary", "arbitrary"), VMEM_BUDGET),
        name="mm",
    )(a, w)


def _mm_t_kernel(*refs, scaled):
    a_ref, w_ref = refs[0], refs[1]
    s_ref = refs[2] if scaled else None
    o_ref, wb_ref = refs[-2], refs[-1]
    tm, tn = o_ref.shape

    @pl.when(pl.program_id(1) == 0)
    def _():
        _cast_panel(w_ref, wb_ref)

    for c in range(tm // DOT_ROWS_T):
        rows = slice(c * DOT_ROWS_T, (c + 1) * DOT_ROWS_T)
        res = _dot(a_ref[rows, :], wb_ref[...], NT)
        if scaled:
            res = res * s_ref[...]
        o_ref[rows, :] = res.astype(o_ref.dtype)


def _mm_t(a, wt, layer, *, row0, n_cols, out_dtype, tm=1024, tn=512, col_scale=None):
    m_dim, k_dim = a.shape
    tm = min(tm, m_dim)
    operands = [a, wt]
    in_specs = [pl.BlockSpec((tm, k_dim), lambda j, i: (i, 0)),
                pl.BlockSpec((None, pl.Element(tn), pl.Element(k_dim)),
                             lambda j, i: (layer, pl.multiple_of(row0 + j * tn, 32), 0))]
    if col_scale is not None:
        operands.append(col_scale)
        in_specs.append(pl.BlockSpec((1, tn), lambda j, i: (0, j)))
    return pl.pallas_call(
        functools.partial(_mm_t_kernel, scaled=col_scale is not None),
        grid=(n_cols // tn, m_dim // tm),
        in_specs=in_specs,
        out_specs=pl.BlockSpec((tm, tn), lambda j, i: (i, j)),
        out_shape=jax.ShapeDtypeStruct((m_dim, n_cols), out_dtype),
        scratch_shapes=[pltpu.VMEM((tn, k_dim), BF16)],
        compiler_params=_cparams(("arbitrary", "arbitrary"), VMEM_BUDGET),
        name="mm_t",
    )(*operands)


def _ln_kernel(x_ref, y_ref, g_ref, b_ref, of_ref, ob_ref):
    z = DEEPNORM_ALPHA * x_ref[...] + y_ref[...]
    mu = jnp.mean(z, axis=-1, keepdims=True)
    zc = z - mu
    var = jnp.mean(zc * zc, axis=-1, keepdims=True)
    out = zc * lax.rsqrt(var + LN_EPS) * g_ref[...] + b_ref[...]
    of_ref[...] = out
    ob_ref[...] = out.astype(BF16)


def _ln(x, y, g, b):
    t_dim = x.shape[0]
    tm = min(128, t_dim)
    row = pl.BlockSpec((tm, D_MODEL), lambda i: (i, 0))
    vec = pl.BlockSpec((1, D_MODEL), lambda i: (0, 0))
    return pl.pallas_call(
        _ln_kernel, grid=(t_dim // tm,),
        in_specs=[row, row, vec, vec], out_specs=[row, row],
        out_shape=[jax.ShapeDtypeStruct((t_dim, D_MODEL), F32),
                   jax.ShapeDtypeStruct((t_dim, D_MODEL), BF16)],
        compiler_params=_cparams(("arbitrary",)),
        name="ln",
    )(x, y, g.reshape(1, D_MODEL), b.reshape(1, D_MODEL))


def _pool_kernel(u_ref, h_ref, pw_ref, ps_ref, o_ref, *, ts, tiles_per_seq):
    it = pl.program_id(0) % tiles_per_seq
    x = u_ref[...].astype(F32)
    halo = jnp.where(it == 0, 0.0, h_ref[...].astype(F32))
    xe = jnp.concatenate([halo, x], axis=0)
    t_seq = lax.broadcasted_iota(jnp.int32, (ts, 1), 0) + it * ts
    for g, win in enumerate(POOL_WINDOWS):
        cols = slice(g * POOL_GROUP_DIM, (g + 1) * POOL_GROUP_DIM)
        s = xe[:, cols]
        span = 1
        while span < win:
            s = s + pltpu.roll(s, span, axis=0)
            span *= 2
        cnt = jnp.minimum(t_seq + 1, win).astype(F32)
        d = s[POOL_HALO:, :] / cnt - x[:, cols]
        y = _dot(d.astype(BF16), pw_ref[g].astype(BF16))
        o_ref[:, cols] = (y * ps_ref[:, cols]).astype(o_ref.dtype)


def _pool(main, pool_w, pool_scale, seq):
    t_dim = main.shape[0]
    ts = min(512, seq)
    hb = ts // POOL_HALO
    return pl.pallas_call(
        functools.partial(_pool_kernel, ts=ts, tiles_per_seq=seq // ts),
        grid=(t_dim // ts,),
        in_specs=[pl.BlockSpec((ts, POOL_WIDTH), lambda i: (i, 0)),
                  pl.BlockSpec((POOL_HALO, POOL_WIDTH), lambda i: (jnp.maximum(i * hb - 1, 0), 0)),
                  pl.BlockSpec((4, POOL_GROUP_DIM, POOL_GROUP_DIM), lambda i: (0, 0, 0)),
                  pl.BlockSpec((1, POOL_WIDTH), lambda i: (0, 0))],
        out_specs=pl.BlockSpec((ts, POOL_WIDTH), lambda i: (i, 0)),
        out_shape=jax.ShapeDtypeStruct((t_dim, POOL_WIDTH), BF16),
        compiler_params=_cparams(("arbitrary",)),
        name="pool",
    )(main, main, pool_w, pool_scale.reshape(1, POOL_WIDTH))


def _attn_kernel(q_ref, k_ref, v_ref, r_ref, o_ref, *, tq, tk):
    iq = pl.program_id(2)
    rmat = r_ref[...]

    def front(items):
        starts = [pl.multiple_of(j * tk, tk) for _, _, j, _ in items]
        z = [_dot(q, k_ref[pl.ds(r0, tk), :], NT) for (q, _, _, _), r0 in zip(items, starts)]
        nz = [-zj for zj in z]
        lk = [jnp.minimum(m, 0.0) - jnp.log2(1.0 + jnp.exp2(jnp.minimum(zj, m)))
              for zj, m in zip(z, nz)]
        causal = []
        for b, (q, row0, j, masked) in enumerate(items):
            causal.append(None)
            if masked:
                shape = (q.shape[0], tk)
                causal[b] = (lax.broadcasted_iota(jnp.int32, shape, 1) + j * tk
                             < lax.broadcasted_iota(jnp.int32, shape, 0) + row0)
                lk[b] = jnp.where(causal[b], lk[b], 0.0)
        later = [_dot(x.astype(BF16), rmat) for x in lk]
        base = [zj + lt for zj, lt in zip(z, later)]
        return list(zip(base, lk, causal, starts))

    def back(done, acc, run):
        for base, lk, causal, r0 in done:
            att = jnp.exp2(base + run)
            if causal is not None:
                att = jnp.where(causal, att, 0.0)
            acc = acc + _dot(att.astype(BF16), v_ref[pl.ds(r0, tk), :])
            run = run + jnp.sum(lk, axis=1, keepdims=True)
        return acc, run

    zero_acc = jnp.zeros((tk, SB_HEAD_DIM), F32)
    zero_run = jnp.zeros((tk, 1), F32)
    q_top, q_bot = q_ref[:tk, :], q_ref[tk:, :]
    row_top = iq * tq
    diag = front([(q_top, row_top, 2 * iq, True), (q_bot, row_top + tk, 2 * iq + 1, True),
                  (q_bot, row_top + tk, 2 * iq, False)])
    acc_t, run_t = back(diag[:1], zero_acc, zero_run)
    acc_b, run_b = back(diag[1:], zero_acc, zero_run)
    acc = jnp.concatenate([acc_t, acc_b], axis=0)
    run = jnp.concatenate([run_t, run_b], axis=0)
    q = q_ref[...]

    def pair(n, carry):
        j = 2 * iq - 1 - 2 * n
        return back(front([(q, 0, j, False), (q, 0, j - 1, False)]), *carry)

    acc, run = lax.fori_loop(0, iq, pair, (acc, run))
    o_ref[...] = acc.astype(o_ref.dtype)


def _attention(main, batch, seq):
    t_dim = main.shape[0]
    tq = min(512, seq)
    tk = tq // 2
    nq = seq // tq
    q_blk0 = POOL_WIDTH // LANES
    k_blk0 = q_blk0 + SB_HEADS
    v_blk0 = k_blk0 + SB_HEADS
    rmat = (lax.broadcasted_iota(jnp.int32, (tk, tk), 0)
            >= lax.broadcasted_iota(jnp.int32, (tk, tk), 1)).astype(BF16)
    return pl.pallas_call(
        functools.partial(_attn_kernel, tq=tq, tk=tk),
        grid=(batch, SB_HEADS, nq),
        in_specs=[pl.BlockSpec((tq, SB_HEAD_DIM), lambda b, h, i: (b * nq + i, q_blk0 + h)),
                  pl.BlockSpec((seq, SB_HEAD_DIM), lambda b, h, i: (b, k_blk0 + h)),
                  pl.BlockSpec((seq, SB_HEAD_DIM), lambda b, h, i: (b, v_blk0 + h)),
                  pl.BlockSpec((tk, tk), lambda b, h, i: (0, 0))],
        out_specs=pl.BlockSpec((tq, SB_HEAD_DIM), lambda b, h, i: (b * nq + i, h)),
        out_shape=jax.ShapeDtypeStruct((t_dim, SB_WIDTH), BF16),
        compiler_params=_cparams(("arbitrary", "arbitrary", "arbitrary")),
        name="attn",
    )(main, main, main, rmat)


def _head_sum(x, bd):
    hi, lo = _split(x)
    return jnp.concatenate(
        [_dot(hi[:, c * LANES:(c + 1) * LANES], bd) + _dot(lo[:, c * LANES:(c + 1) * LANES], bd)
         for c in range(x.shape[1] // LANES)], axis=1)


def _rwkv_prep_kernel(u_ref, h_ref, mu_ref, w0_ref, a0_ref, kk_ref, ka_ref, wch_ref, wcl_ref, bd_ref,
                      r_o, k_o, v_o, lw_o, an_o, bn_o, g_o, *, ts, tiles_per_seq):
    it = pl.program_id(0) % tiles_per_seq
    u = u_ref[...].astype(F32)
    last = jnp.where(it == 0, 0.0, h_ref[RWKV_HALO - 1:RWKV_HALO, :].astype(F32))
    row = lax.broadcasted_iota(jnp.int32, (ts, 1), 0)
    prev = jnp.where(row == 0, last, pltpu.roll(u, 1, axis=0))
    x = u + (prev - u) * mu_ref[...]
    w_dim = RWKV_WIDTH
    r, k, v = x[:, :w_dim], x[:, w_dim:2 * w_dim], x[:, 2 * w_dim:3 * w_dim]
    low = x[:, 3 * w_dim:3 * w_dim + RWKV_LOW]
    lane = lax.broadcasted_iota(jnp.int32, (1, RWKV_LOW), 1)
    low = jnp.where(lane < RWKV_DECAY_RANK, jnp.tanh(low),
                    jnp.where(lane < RWKV_DECAY_RANK + RWKV_A_RANK, low, jax.nn.sigmoid(low)))
    mm = _dot3s(low, (wch_ref[...], wcl_ref[...]))
    w_lin = w0_ref[...] + mm[:, :w_dim]
    soft = jnp.maximum(-w_lin, 0.0) + jnp.log(1.0 + jnp.exp(-jnp.abs(w_lin)))
    log_decay = -jnp.exp(-soft - 0.5)
    a = jax.nn.sigmoid(a0_ref[...] + mm[:, w_dim:2 * w_dim])
    g = mm[:, 2 * w_dim:]
    kk = k * kk_ref[...]
    kk = kk * lax.rsqrt(jnp.maximum(_head_sum(kk * kk, bd_ref[...]), 1e-24))
    r_o[...] = r
    k_o[...] = k * (1.0 + (a - 1.0) * ka_ref[...])
    v_o[...] = v
    lw_o[...] = log_decay
    an_o[...] = -kk
    bn_o[...] = kk * a
    g_o[...] = g


def _rwkv_prep(urw, seq, mu, w0, a0, k_k, k_a, wc, bd):
    t_dim = urw.shape[0]
    wc_hi, wc_lo = _split(wc)
    ts = min(256, seq)
    hb = ts // RWKV_HALO
    wide = pl.BlockSpec((ts, RWKV_PAD), lambda i: (i, 0))
    halo = pl.BlockSpec((RWKV_HALO, RWKV_PAD), lambda i: (jnp.maximum(i * hb - 1, 0), 0))
    vec = pl.BlockSpec((1, RWKV_WIDTH), lambda i: (0, 0))
    out = pl.BlockSpec((ts, RWKV_WIDTH), lambda i: (i, 0))
    return pl.pallas_call(
        functools.partial(_rwkv_prep_kernel, ts=ts, tiles_per_seq=seq // ts),
        grid=(t_dim // ts,),
        in_specs=[wide, halo, pl.BlockSpec((1, RWKV_PAD), lambda i: (0, 0)), vec, vec, vec, vec,
                  pl.BlockSpec((RWKV_LOW, 3 * RWKV_WIDTH), lambda i: (0, 0)),
                  pl.BlockSpec((RWKV_LOW, 3 * RWKV_WIDTH), lambda i: (0, 0)),
                  pl.BlockSpec((LANES, LANES), lambda i: (0, 0))],
        out_specs=[out] * 7,
        out_shape=[jax.ShapeDtypeStruct((t_dim, RWKV_WIDTH), F32)] * 7,
        compiler_params=_cparams(("arbitrary",), VMEM_BUDGET),
        name="rwkv_prep",
    )(urw, urw, mu, w0, a0, k_k, k_a, wc_hi, wc_lo, bd)


def _rwkv_chunk_kernel(r_ref, k_ref, v_ref, lw_ref, an_ref, bn_ref,
                       m_o, n_o, q_o, y_o, *, pairs):
    c_len = RWKV_CHUNK
    n2 = 2 * c_len
    ri = lax.broadcasted_iota(jnp.int32, (n2, n2), 0)
    ci = lax.broadcasted_iota(jnp.int32, (n2, n2), 1)
    strict = (ri % c_len) > (ci % c_len)
    incl = (ri % c_len) >= (ci % c_len)
    eye = (ri == ci).astype(F32)
    tri = (lax.broadcasted_iota(jnp.int32, (c_len, c_len), 0)
           >= lax.broadcasted_iota(jnp.int32, (c_len, c_len), 1)).astype(F32)
    lane = lax.broadcasted_iota(jnp.int32, (n2, LANES), 1)
    rowh = lax.broadcasted_iota(jnp.int32, (n2, LANES), 0)
    own = (lane < RWKV_HEAD_DIM) == (rowh < c_len)

    def stack(x):
        return jnp.where(own, jnp.concatenate([x, x], axis=0), 0.0)

    cum_all = _dot32(tri, lw_ref[...])
    every = range(pairs)
    sl = [slice(p * LANES, (p + 1) * LANES) for p in every]
    cum = [cum_all[:, s] for s in sl]
    tot = [c[c_len - 1:c_len, :] for c in cum]
    r_t = [stack(r_ref[:, sl[p]] * jnp.exp(cum[p])) for p in every]
    a_t = [stack(an_ref[:, sl[p]] * jnp.exp(cum[p] - lw_ref[:, sl[p]])).astype(BF16) for p in every]
    b_t = [stack(bn_ref[:, sl[p]] * jnp.exp(-cum[p])).astype(BF16) for p in every]
    k_t = [stack(k_ref[:, sl[p]] * jnp.exp(-cum[p])).astype(BF16) for p in every]
    b_e = [stack(bn_ref[:, sl[p]] * jnp.exp(tot[p] - cum[p])).astype(BF16) for p in every]
    k_e = [stack(k_ref[:, sl[p]] * jnp.exp(tot[p] - cum[p])).astype(BF16) for p in every]
    v_s = [stack(v_ref[:, sl[p]]).astype(BF16) for p in every]
    gram = [_dot(jnp.concatenate([a_t[p], r_t[p].astype(BF16)], axis=0),
                 jnp.concatenate([b_t[p], k_t[p]], axis=0), NT) for p in every]
    a_ab = [jnp.where(strict, g[:n2, :n2], 0.0) for g in gram]
    a_ak = [jnp.where(strict, g[:n2, n2:], 0.0).astype(BF16) for g in gram]
    r_ab = [jnp.where(incl, g[n2:, :n2], 0.0).astype(BF16) for g in gram]
    r_ak = [jnp.where(incl, g[n2:, n2:], 0.0).astype(BF16) for g in gram]
    inv = [eye + a for a in a_ab]
    pw = [a.astype(BF16) for a in a_ab]
    pw = [_dot(a, a) for a in pw]
    akv = [_dot(a_ak[p], v_s[p]).astype(BF16) for p in every]
    span = 2
    while 2 * span < c_len:
        pwb = [a.astype(BF16) for a in pw]
        both = [_dot(pwb[p], jnp.concatenate([pwb[p], inv[p].astype(BF16)], axis=1)) for p in every]
        pw = [b[:, :n2] for b in both]
        inv = [inv[p] + both[p][:, n2:] for p in every]
        span *= 2
    inv = [inv[p] + _dot(pw[p].astype(BF16), inv[p].astype(BF16)) for p in every]
    wx = [_dot(inv[p].astype(BF16), jnp.concatenate([a_t[p], akv[p]], axis=1)).astype(BF16)
          for p in every]
    qy = [_dot(r_ab[p], wx[p]) for p in every]
    rv = [_dot(r_ak[p], v_s[p]) for p in every]
    mn = [_dot(b_e[p], wx[p], TN) for p in every]
    kv = [_dot(k_e[p], v_s[p], TN) for p in every]
    for p in every:
        q_s = r_t[p] + qy[p][:, :n2]
        y_s = qy[p][:, n2:] + rv[p]
        m_o[0, p] = eye * jnp.exp(tot[p]) + mn[p][:, :n2]
        n_o[0, p] = mn[p][:, n2:] + kv[p]
        q_o[:, sl[p]] = q_s[:c_len] + q_s[c_len:]
        y_o[:, sl[p]] = y_s[:c_len] + y_s[c_len:]


def _rwkv_chunks(r, k, v, lw, an, bn):
    t_dim = r.shape[0]
    pairs = RWKV_PAIRS
    n_chunks = t_dim // RWKV_CHUNK
    tile = pl.BlockSpec((RWKV_CHUNK, pairs * LANES), lambda c, p: (c, p))
    op = pl.BlockSpec((1, pairs, LANES, LANES), lambda c, p: (c, p, 0, 0))
    op_shape = jax.ShapeDtypeStruct((n_chunks, RWKV_PAIRS, LANES, LANES), F32)
    row_shape = jax.ShapeDtypeStruct((t_dim, RWKV_WIDTH), F32)
    return pl.pallas_call(
        functools.partial(_rwkv_chunk_kernel, pairs=pairs),
        grid=(n_chunks, RWKV_PAIRS // pairs),
        in_specs=[tile] * 6,
        out_specs=[op, op, tile, tile],
        out_shape=[op_shape, op_shape, row_shape, row_shape],
        compiler_params=_cparams(("arbitrary", "arbitrary")),
        name="rwkv_chunk",
    )(r, k, v, lw, an, bn)


def _rwkv_sweep_kernel(m_ref, n_ref, q_ref, y0_ref, r_ref, k_ref, v_ref, g_ref,
                       rk_ref, lg_ref, lb_ref, bd_ref, o_ref, z_ref, y_ref, *, pairs, chunks):
    @pl.when(pl.program_id(1) == 0)
    def _():
        z_ref[...] = jnp.zeros_like(z_ref)

    sl = [slice(p * LANES, (p + 1) * LANES) for p in range(pairs)]
    z = [z_ref[p] for p in range(pairs)]
    for c in range(chunks):
        rows = slice(c * RWKV_CHUNK, (c + 1) * RWKV_CHUNK)
        zs = [_split(zp) for zp in z]
        ys = [_dot3s(q_ref[rows, sl[p]], zs[p]) for p in range(pairs)]
        zn = [_dot3s(m_ref[c, p], zs[p]) for p in range(pairs)]
        for p in range(pairs):
            y_ref[rows, sl[p]] = ys[p] + y0_ref[rows, sl[p]]
        z = [zn[p] + n_ref[c, p] for p in range(pairs)]
    for p in range(pairs):
        z_ref[p] = z[p]

    bd = bd_ref[...]
    y = y_ref[...]
    mu = _head_sum(y, bd) * (1.0 / RWKV_HEAD_DIM)
    yc = y - mu
    var = _head_sum(yc * yc, bd) * (1.0 / RWKV_HEAD_DIM)
    yn = yc * lax.rsqrt(var + RWKV_GN_EPS) * lg_ref[...] + lb_ref[...]
    bonus = _head_sum(r_ref[...] * k_ref[...] * rk_ref[...], bd) * v_ref[...]
    o_ref[...] = ((yn + bonus) * g_ref[...]).astype(o_ref.dtype)


def _rwkv_sweep(m_op, n_op, q_op, y0, r, k, v, g, r_k, ln_g, ln_b, bd, batch, seq):
    t_dim = q_op.shape[0]
    pairs = RWKV_PAIRS
    chunks = min(4, seq // RWKV_CHUNK)
    steps = seq // (RWKV_CHUNK * chunks)
    rows = RWKV_CHUNK * chunks
    tile = pl.BlockSpec((rows, RWKV_WIDTH), lambda b, s: (b * steps + s, 0))
    op = pl.BlockSpec((chunks, pairs, LANES, LANES), lambda b, s: (b * steps + s, 0, 0, 0))
    vec = pl.BlockSpec((1, RWKV_WIDTH), lambda b, s: (0, 0))
    return pl.pallas_call(
        functools.partial(_rwkv_sweep_kernel, pairs=pairs, chunks=chunks),
        grid=(batch, steps),
        in_specs=[op, op] + [tile] * 6 + [vec] * 3 + [pl.BlockSpec((LANES, LANES), lambda b, s: (0, 0))],
        out_specs=tile,
        out_shape=jax.ShapeDtypeStruct((t_dim, RWKV_WIDTH), BF16),
        scratch_shapes=[pltpu.VMEM((pairs, LANES, LANES), F32), pltpu.VMEM((rows, RWKV_WIDTH), F32)],
        compiler_params=_cparams(("arbitrary", "arbitrary"), VMEM_BUDGET),
        name="rwkv_sweep",
    )(m_op, n_op, q_op, y0, r, k, v, g, r_k, ln_g, ln_b, bd)


def _merge_kernel(yp_ref, ya_ref, yr_ref, wp_ref, wa_ref, wr_ref, g0_ref, g1_ref, g2_ref,
                  b0_ref, b1_ref, b2_ref, o_ref, wpb, wab, wrb):
    @pl.when(pl.program_id(1) == 0)
    def _():
        wpb[...] = wp_ref[...].astype(BF16)
        wab[...] = wa_ref[...].astype(BF16)
        wrb[...] = wr_ref[...].astype(BF16)

    m = jax.nn.sigmoid(g0_ref[...] + b0_ref[...]) * _dot(yp_ref[...], wpb[...])
    m = m + jax.nn.sigmoid(g1_ref[...] + b1_ref[...]) * _dot(ya_ref[...], wab[...])
    m = m + jax.nn.sigmoid(g2_ref[...] + b2_ref[...]) * _dot(yr_ref[...], wrb[...])
    o_ref[...] = m.astype(o_ref.dtype)


def _merge(y_pool, y_attn, y_rwkv, w_pool, w_attn, w_rwkv, layer, gate_logits, b_gate):
    t_dim = y_pool.shape[0]
    tm = min(1024, t_dim)
    tn = 512
    nb = D_MODEL // tn
    act = lambda width: pl.BlockSpec((tm, width), lambda j, i: (i, 0))
    wgt = lambda width: pl.BlockSpec((None, width, tn), lambda j, i: (layer, 0, j))
    gate = lambda br: pl.BlockSpec((tm, tn), lambda j, i: (i, br * nb + j))
    bias = lambda br: pl.BlockSpec((1, tn), lambda j, i: (0, br * nb + j))
    return pl.pallas_call(
        _merge_kernel, grid=(nb, t_dim // tm),
        in_specs=[act(POOL_WIDTH), act(SB_WIDTH), act(RWKV_WIDTH),
                  wgt(POOL_WIDTH), wgt(SB_WIDTH), wgt(RWKV_WIDTH),
                  gate(0), gate(1), gate(2), bias(0), bias(1), bias(2)],
        out_specs=pl.BlockSpec((tm, tn), lambda j, i: (i, j)),
        out_shape=jax.ShapeDtypeStruct((t_dim, D_MODEL), BF16),
        scratch_shapes=[pltpu.VMEM((POOL_WIDTH, tn), BF16), pltpu.VMEM((SB_WIDTH, tn), BF16),
                        pltpu.VMEM((RWKV_WIDTH, tn), BF16)],
        compiler_params=_cparams(("arbitrary", "arbitrary"), VMEM_BUDGET),
        name="merge",
    )(y_pool, y_attn, y_rwkv, w_pool, w_attn, w_rwkv, gate_logits, gate_logits, gate_logits,
      b_gate, b_gate, b_gate)


def _ffn_up_kernel(a_ref, wa_ref, wl_ref, cw_ref, cb_ref, o_ref, wb_ref, halo_ref, *, tiles_per_seq):
    i = pl.program_id(1)
    tm, tn = o_ref.shape

    @pl.when(i == 0)
    def _():
        _cast_panel(wa_ref, wb_ref.at[:, :tn])
        _cast_panel(wl_ref, wb_ref.at[:, tn:])
        halo_ref[...] = jnp.zeros_like(halo_ref)

    prev = jnp.where(i % tiles_per_seq == 0, 0.0, halo_ref[...])
    row = lax.broadcasted_iota(jnp.int32, (DOT_ROWS, 1), 0)
    for c in range(tm // DOT_ROWS):
        rows = slice(c * DOT_ROWS, (c + 1) * DOT_ROWS)
        res = _dot(a_ref[rows, :], wb_ref[...])
        act, lin = res[:, :tn], res[:, tn:]
        p1 = jnp.where(row == 0, prev[7:8, :], pltpu.roll(act, 1, axis=0))
        p2 = jnp.where(row == 0, prev[6:7, :],
                       jnp.where(row == 1, prev[7:8, :], pltpu.roll(act, 2, axis=0)))
        cv = p2 * cw_ref[0:1, :] + p1 * cw_ref[1:2, :] + act * cw_ref[2:3, :] + cb_ref[...]
        gelu = 0.5 * cv * (1.0 + lax.erf(cv * (2.0 ** -0.5)))
        o_ref[rows, :] = (gelu * lin).astype(o_ref.dtype)
        prev = act[DOT_ROWS - 8:, :]
    halo_ref[...] = prev


def _ffn_up(a, w_up, layer, conv_w, conv_b, seq):
    t_dim, k_dim = a.shape
    tm = min(2048, seq)
    tn = 256
    nb = D_FF // tn
    return pl.pallas_call(
        functools.partial(_ffn_up_kernel, tiles_per_seq=seq // tm),
        grid=(nb, t_dim // tm),
        in_specs=[pl.BlockSpec((tm, k_dim), lambda j, i: (i, 0)),
                  pl.BlockSpec((None, k_dim, tn), lambda j, i: (layer, 0, j)),
                  pl.BlockSpec((None, k_dim, tn), lambda j, i: (layer, 0, nb + j)),
                  pl.BlockSpec((3, tn), lambda j, i: (0, j)),
                  pl.BlockSpec((1, tn), lambda j, i: (0, j))],
        out_specs=pl.BlockSpec((tm, tn), lambda j, i: (i, j)),
        out_shape=jax.ShapeDtypeStruct((t_dim, D_FF), BF16),
        scratch_shapes=[pltpu.VMEM((k_dim, 2 * tn), BF16), pltpu.VMEM((8, tn), F32)],
        compiler_params=_cparams(("arbitrary", "arbitrary"), VMEM_BUDGET),
        name="ffn_up",
    )(a, w_up, w_up, conv_w, conv_b.reshape(1, D_FF))


def kernel(x, w_in, b_gate, pool_w, pool_scale, rwkv_mu, rwkv_w0, rwkv_w2, rwkv_a0, rwkv_a2, rwkv_g2, rwkv_k_k, rwkv_k_a, rwkv_r_k, rwkv_ln_g, rwkv_ln_b, w_branch_pool, w_branch_attn, w_branch_rwkv, w_out, ln1_g, ln1_b, w_up, ffn_conv_w, ffn_conv_b, w_down, ln2_g, ln2_b):
    batch, seq, _ = x.shape
    t_dim = batch * seq
    xf = x.reshape(t_dim, D_MODEL)
    xb = xf.astype(BF16)
    half = lax.broadcasted_iota(jnp.int32, (LANES, LANES), 0) // RWKV_HEAD_DIM
    bd = (half == half.T).astype(BF16)
    vec = lambda p: p.reshape(1, -1)
    col = lax.broadcasted_iota(jnp.int32, (1, N_MAIN), 1)
    q_cols = (col >= POOL_WIDTH) & (col < POOL_WIDTH + SB_WIDTH)
    main_scale = jnp.where(q_cols, SB_HEAD_DIM ** -0.5 * math.log2(math.e), 1.0).astype(F32)
    w_in_t = jnp.swapaxes(w_in, 1, 2)
    for l in range(DEPTH):
        main = _mm_t(xb, w_in_t, l, row0=0, n_cols=N_MAIN, out_dtype=BF16, tm=2048,
                     col_scale=main_scale)
        urw = _mm_t(xb, w_in_t, l, row0=N_MAIN, n_cols=RWKV_PAD, out_dtype=BF16, tm=2048)
        gate_logits = _mm_t(xb, w_in_t, l, row0=GATE_OFF, n_cols=3 * D_MODEL, out_dtype=BF16,
                            tm=2048)

        y_pool = _pool(main, pool_w[l], pool_scale[l], seq)
        y_attn = _attention(main, batch, seq)

        wc = jnp.zeros((RWKV_LOW, 3 * RWKV_WIDTH), F32)
        wc = wc.at[:RWKV_DECAY_RANK, :RWKV_WIDTH].set(rwkv_w2[l])
        wc = wc.at[RWKV_DECAY_RANK:RWKV_DECAY_RANK + RWKV_A_RANK, RWKV_WIDTH:2 * RWKV_WIDTH].set(rwkv_a2[l])
        lo = RWKV_DECAY_RANK + RWKV_A_RANK
        wc = wc.at[lo:lo + RWKV_GATE_RANK, 2 * RWKV_WIDTH:].set(rwkv_g2[l])
        mu = jnp.pad(rwkv_mu[l], (0, RWKV_PAD - RWKV_SHIFT_WIDTH)).reshape(1, RWKV_PAD)
        r, k, v, lw, an, bn, g = _rwkv_prep(urw, seq, mu, vec(rwkv_w0[l]), vec(rwkv_a0[l]),
                                            vec(rwkv_k_k[l]), vec(rwkv_k_a[l]), wc, bd)
        m_op, n_op, q_op, y0 = _rwkv_chunks(r, k, v, lw, an, bn)
        y_rwkv = _rwkv_sweep(m_op, n_op, q_op, y0, r, k, v, g, vec(rwkv_r_k[l]),
                             vec(rwkv_ln_g[l]), vec(rwkv_ln_b[l]), bd, batch, seq)

        merged = _merge(y_pool, y_attn, y_rwkv, w_branch_pool, w_branch_attn, w_branch_rwkv, l,
                        gate_logits, vec(b_gate[l]))
        sub = _mm(merged, w_out, l, out_dtype=BF16, tm=2048, tn=512)
        xf, xb = _ln(xf, sub, ln1_g[l], ln1_b[l])

        h = _ffn_up(xb, w_up, l, ffn_conv_w[l], ffn_conv_b[l], seq)
        ffn = _mm(h, w_down, l, out_dtype=BF16, tm=512, tn=512, single_buffer_w=True)
        xf, xb = _ln(xf, ffn, ln2_g[l], ln2_b[l])
    return xf.reshape(batch, seq, D_MODEL)
```

```python
import functools
import math

import jax
import jax.numpy as jnp
from jax import lax
from jax.experimental import pallas as pl
from jax.experimental.pallas import tpu as pltpu

F32 = jnp.float32
BF16 = jnp.bfloat16

D_MODEL = 4096
DEPTH = 2
POOL_WIDTH = 1024
POOL_WINDOWS = (2, 4, 8, 16)
POOL_GROUP_DIM = 256
POOL_HALO = 16
RWKV_HALO = 16
SB_HEAD_DIM = 128
SB_WIDTH = 2048
SB_HEADS = 16
RWKV_WIDTH = 1024
RWKV_HEAD_DIM = 64
RWKV_DECAY_RANK = 64
RWKV_A_RANK = 64
RWKV_GATE_RANK = 160
RWKV_SHIFT_WIDTH = 3 * RWKV_WIDTH + RWKV_DECAY_RANK + RWKV_A_RANK + RWKV_GATE_RANK
RWKV_GN_EPS = 64e-5
N_MAIN = POOL_WIDTH + 3 * SB_WIDTH
RWKV_PAD = 3584
RWKV_LOW = 384
GATE_OFF = N_MAIN + RWKV_SHIFT_WIDTH
D_FF = 11008
DEEPNORM_ALPHA = (2 * DEPTH) ** 0.25
LN_EPS = 1e-5

LANES = 128
RWKV_CHUNK = 64
RWKV_PAIRS = RWKV_WIDTH // LANES
VMEM_BUDGET = 58 * 1024 * 1024
CAST_ROWS = 256
DOT_ROWS = 512
DOT_ROWS_T = 512


def _cparams(sem, vmem=None):
    return pltpu.CompilerParams(dimension_semantics=sem, vmem_limit_bytes=vmem)


def _dot(a, b, dims=None, prec=None):
    if dims is None:
        dims = (((a.ndim - 1,), (0,)), ((), ()))
    return lax.dot_general(a, b, dims, precision=prec, preferred_element_type=F32)


def _dot32(a, b, dims=None):
    return _dot(a, b, dims, prec=lax.Precision.HIGHEST)


def _split(a):
    hi = a.astype(BF16)
    return hi, (a - hi.astype(F32)).astype(BF16)


def _dot3s(a, b_split):
    ah, al = _split(a)
    bh, bl = b_split
    return _dot(ah, bh) + _dot(ah, bl) + _dot(al, bh)


NT = (((1,), (1,)), ((), ()))
TN = (((0,), (0,)), ((), ()))


def _cast_panel(w_ref, wb_ref):
    def cast(c, carry):
        r = pl.multiple_of(c * CAST_ROWS, CAST_ROWS)
        wb_ref[pl.ds(r, CAST_ROWS), :] = w_ref[pl.ds(r, CAST_ROWS), :].astype(BF16)
        return carry
    lax.fori_loop(0, wb_ref.shape[0] // CAST_ROWS, cast, 0)


def _mm_kernel(a_ref, w_ref, o_ref, wb_ref):
    tm = o_ref.shape[0]

    @pl.when(pl.program_id(1) == 0)
    def _():
        _cast_panel(w_ref, wb_ref)

    for c in range(tm // DOT_ROWS):
        rows = slice(c * DOT_ROWS, (c + 1) * DOT_ROWS)
        o_ref[rows, :] = _dot(a_ref[rows, :], wb_ref[...]).astype(o_ref.dtype)


def _mm(a, w, layer, *, out_dtype, tm, tn, single_buffer_w=False):
    m_dim, k_dim = a.shape
    n_cols = w.shape[2]
    tm = min(tm, m_dim)
    w_mode = dict(pipeline_mode=pl.Buffered(1)) if single_buffer_w else {}
    return pl.pallas_call(
        _mm_kernel,
        grid=(n_cols // tn, m_dim // tm),
        in_specs=[pl.BlockSpec((tm, k_dim), lambda j, i: (i, 0)),
                  pl.BlockSpec((None, k_dim, tn), lambda j, i: (layer, 0, j), **w_mode)],
        out_specs=pl.BlockSpec((tm, tn), lambda j, i: (i, j)),
        out_shape=jax.ShapeDtypeStruct((m_dim, n_cols), out_dtype),
        scratch_shapes=[pltpu.VMEM((k_dim, tn), BF16)],
        compiler_params=_cparams(("arbitrary", "arbitrary"), VMEM_BUDGET),
        name="mm",
    )(a, w)


def _mm_t_kernel(*refs, scaled):
    a_ref, w_ref = refs[0], refs[1]
    s_ref = refs[2] if scaled else None
    o_ref, wb_ref = refs[-2], refs[-1]
    tm, tn = o_ref.shape

    @pl.when(pl.program_id(1) == 0)
    def _():
        _cast_panel(w_ref, wb_ref)

    for c in range(tm // DOT_ROWS_T):
        rows = slice(c * DOT_ROWS_T, (c + 1) * DOT_ROWS_T)
        res = _dot(a_ref[rows, :], wb_ref[...], NT)
        if scaled:
            res = res * s_ref[...]
        o_ref[rows, :] = res.astype(o_ref.dtype)


def _mm_t(a, wt, layer, *, row0, n_cols, out_dtype, tm=1024, tn=512, col_scale=None):
    m_dim, k_dim = a.shape
    tm = min(tm, m_dim)
    operands = [a, wt]
    in_specs = [pl.BlockSpec((tm, k_dim), lambda j, i: (i, 0)),
                pl.BlockSpec((None, pl.Element(tn), pl.Element(k_dim)),
                             lambda j, i: (layer, pl.multiple_of(row0 + j * tn, 32), 0))]
    if col_scale is not None:
        operands.append(col_scale)
        in_specs.append(pl.BlockSpec((1, tn), lambda j, i: (0, j)))
    return pl.pallas_call(
        functools.partial(_mm_t_kernel, scaled=col_scale is not None),
        grid=(n_cols // tn, m_dim // tm),
        in_specs=in_specs,
        out_specs=pl.BlockSpec((tm, tn), lambda j, i: (i, j)),
        out_shape=jax.ShapeDtypeStruct((m_dim, n_cols), out_dtype),
        scratch_shapes=[pltpu.VMEM((tn, k_dim), BF16)],
        compiler_params=_cparams(("arbitrary", "arbitrary"), VMEM_BUDGET),
        name="mm_t",
    )(*operands)


def _ln_kernel(x_ref, y_ref, g_ref, b_ref, of_ref, *maybe_ob_ref):
    z = DEEPNORM_ALPHA * x_ref[...] + y_ref[...]
    mu = jnp.mean(z, axis=-1, keepdims=True)
    zc = z - mu
    var = jnp.mean(zc * zc, axis=-1, keepdims=True)
    out = zc * lax.rsqrt(var + LN_EPS) * g_ref[...] + b_ref[...]
    of_ref[...] = out
    for ob_ref in maybe_ob_ref:
        ob_ref[...] = out.astype(BF16)


def _ln(x, y, g, b, *, with_bf16):
    t_dim = x.shape[0]
    tm = min(256, t_dim)
    row = pl.BlockSpec((tm, D_MODEL), lambda i: (i, 0))
    vec = pl.BlockSpec((1, D_MODEL), lambda i: (0, 0))
    out_shape = [jax.ShapeDtypeStruct((t_dim, D_MODEL), F32)]
    if with_bf16:
        out_shape.append(jax.ShapeDtypeStruct((t_dim, D_MODEL), BF16))
    return pl.pallas_call(
        _ln_kernel, grid=(t_dim // tm,),
        in_specs=[row, row, vec, vec], out_specs=[row] * len(out_shape),
        out_shape=out_shape,
        compiler_params=_cparams(("arbitrary",), VMEM_BUDGET),
        name="ln",
    )(x, y, g.reshape(1, D_MODEL), b.reshape(1, D_MODEL))


def _pool_kernel(u_ref, h_ref, pw_ref, ps_ref, o_ref, *, ts, tiles_per_seq):
    it = pl.program_id(0) % tiles_per_seq
    x = u_ref[...].astype(F32)
    halo = jnp.where(it == 0, 0.0, h_ref[...].astype(F32))
    xe = jnp.concatenate([halo, x], axis=0)
    t_seq = lax.broadcasted_iota(jnp.int32, (ts, 1), 0) + it * ts
    for g, win in enumerate(POOL_WINDOWS):
        cols = slice(g * POOL_GROUP_DIM, (g + 1) * POOL_GROUP_DIM)
        s = xe[:, cols]
        span = 1
        while span < win:
            s = s + pltpu.roll(s, span, axis=0)
            span *= 2
        cnt = jnp.minimum(t_seq + 1, win).astype(F32)
        d = s[POOL_HALO:, :] / cnt - x[:, cols]
        y = _dot(d.astype(BF16), pw_ref[g].astype(BF16))
        o_ref[:, cols] = (y * ps_ref[:, cols]).astype(o_ref.dtype)


def _pool(main, pool_w, pool_scale, seq):
    t_dim = main.shape[0]
    ts = min(512, seq)
    hb = ts // POOL_HALO
    return pl.pallas_call(
        functools.partial(_pool_kernel, ts=ts, tiles_per_seq=seq // ts),
        grid=(t_dim // ts,),
        in_specs=[pl.BlockSpec((ts, POOL_WIDTH), lambda i: (i, 0)),
                  pl.BlockSpec((POOL_HALO, POOL_WIDTH), lambda i: (jnp.maximum(i * hb - 1, 0), 0)),
                  pl.BlockSpec((4, POOL_GROUP_DIM, POOL_GROUP_DIM), lambda i: (0, 0, 0)),
                  pl.BlockSpec((1, POOL_WIDTH), lambda i: (0, 0))],
        out_specs=pl.BlockSpec((ts, POOL_WIDTH), lambda i: (i, 0)),
        out_shape=jax.ShapeDtypeStruct((t_dim, POOL_WIDTH), BF16),
        compiler_params=_cparams(("arbitrary",)),
        name="pool",
    )(main, main, pool_w, pool_scale.reshape(1, POOL_WIDTH))


def _attn_kernel(q_ref, k_ref, v_ref, r_ref, o_ref, *, tq, tk):
    iq = pl.program_id(2)
    rmat = r_ref[...]

    def front(items):
        starts = [pl.multiple_of(j * tk, tk) for _, _, j, _ in items]
        z = [_dot(q, k_ref[pl.ds(r0, tk), :], NT) for (q, _, _, _), r0 in zip(items, starts)]
        nz = [-zj for zj in z]
        lk = [jnp.minimum(m, 0.0) - jnp.log2(1.0 + jnp.exp2(jnp.minimum(zj, m)))
              for zj, m in zip(z, nz)]
        causal = []
        for b, (q, row0, j, masked) in enumerate(items):
            causal.append(None)
            if masked:
                shape = (q.shape[0], tk)
                causal[b] = (lax.broadcasted_iota(jnp.int32, shape, 1) + j * tk
                             < lax.broadcasted_iota(jnp.int32, shape, 0) + row0)
                lk[b] = jnp.where(causal[b], lk[b], 0.0)
        later = [_dot(x.astype(BF16), rmat) for x in lk]
        base = [zj + lt for zj, lt in zip(z, later)]
        return list(zip(base, lk, causal, starts))

    def back(done, acc, run):
        for base, lk, causal, r0 in done:
            att = jnp.exp2(base + run)
            if causal is not None:
                att = jnp.where(causal, att, 0.0)
            acc = acc + _dot(att.astype(BF16), v_ref[pl.ds(r0, tk), :])
            run = run + jnp.sum(lk, axis=1, keepdims=True)
        return acc, run

    zero_acc = jnp.zeros((tk, SB_HEAD_DIM), F32)
    zero_run = jnp.zeros((tk, 1), F32)
    ratio = tq // tk
    items = []
    for r in range(ratio):
        q_r, row_r = q_ref[r * tk:(r + 1) * tk, :], iq * tq + r * tk
        items += [(q_r, row_r, ratio * iq + s, s == r) for s in range(r, -1, -1)]
    diag = front(items)
    parts, first = [], 0
    for r in range(ratio):
        parts.append(back(diag[first:first + r + 1], zero_acc, zero_run))
        first += r + 1
    acc = jnp.concatenate([p[0] for p in parts], axis=0)
    run = jnp.concatenate([p[1] for p in parts], axis=0)
    q = q_ref[...]

    def pair(n, carry):
        j = ratio * iq - 1 - 2 * n
        return back(front([(q, 0, j, False), (q, 0, j - 1, False)]), *carry)

    acc, run = lax.fori_loop(0, ratio * iq // 2, pair, (acc, run))
    o_ref[...] = acc.astype(o_ref.dtype)


def _attention(main, batch, seq):
    t_dim = main.shape[0]
    tk = min(256, seq // 2)
    tq = 2 * tk
    nq = seq // tq
    q_blk0 = POOL_WIDTH // LANES
    k_blk0 = q_blk0 + SB_HEADS
    v_blk0 = k_blk0 + SB_HEADS
    rmat = (lax.broadcasted_iota(jnp.int32, (tk, tk), 0)
            >= lax.broadcasted_iota(jnp.int32, (tk, tk), 1)).astype(BF16)
    return pl.pallas_call(
        functools.partial(_attn_kernel, tq=tq, tk=tk),
        grid=(batch, SB_HEADS, nq),
        in_specs=[pl.BlockSpec((tq, SB_HEAD_DIM), lambda b, h, i: (b * nq + i, q_blk0 + h)),
                  pl.BlockSpec((seq, SB_HEAD_DIM), lambda b, h, i: (b, k_blk0 + h)),
                  pl.BlockSpec((seq, SB_HEAD_DIM), lambda b, h, i: (b, v_blk0 + h)),
                  pl.BlockSpec((tk, tk), lambda b, h, i: (0, 0))],
        out_specs=pl.BlockSpec((tq, SB_HEAD_DIM), lambda b, h, i: (b * nq + i, h)),
        out_shape=jax.ShapeDtypeStruct((t_dim, SB_WIDTH), BF16),
        compiler_params=_cparams(("arbitrary", "arbitrary", "arbitrary")),
        name="attn",
    )(main, main, main, rmat)


def _head_sum(x, bd):
    hi, lo = _split(x)
    return jnp.concatenate(
        [_dot(hi[:, c * LANES:(c + 1) * LANES], bd) + _dot(lo[:, c * LANES:(c + 1) * LANES], bd)
         for c in range(x.shape[1] // LANES)], axis=1)


def _rwkv_prep_kernel(u_ref, h_ref, mu_ref, w0_ref, a0_ref, kk_ref, ka_ref, wch_ref, wcl_ref, bd_ref,
                      r_o, k_o, v_o, lw_o, an_o, bn_o, g_o, *, ts, tiles_per_seq):
    it = pl.program_id(0) % tiles_per_seq
    u = u_ref[...].astype(F32)
    last = jnp.where(it == 0, 0.0, h_ref[RWKV_HALO - 1:RWKV_HALO, :].astype(F32))
    row = lax.broadcasted_iota(jnp.int32, (ts, 1), 0)
    prev = jnp.where(row == 0, last, pltpu.roll(u, 1, axis=0))
    x = u + (prev - u) * mu_ref[...]
    w_dim = RWKV_WIDTH
    r, k, v = x[:, :w_dim], x[:, w_dim:2 * w_dim], x[:, 2 * w_dim:3 * w_dim]
    low = x[:, 3 * w_dim:3 * w_dim + RWKV_LOW]
    lane = lax.broadcasted_iota(jnp.int32, (1, RWKV_LOW), 1)
    low = jnp.where(lane < RWKV_DECAY_RANK, jnp.tanh(low),
                    jnp.where(lane < RWKV_DECAY_RANK + RWKV_A_RANK, low, jax.nn.sigmoid(low)))
    mm = _dot3s(low, (wch_ref[...], wcl_ref[...]))
    w_lin = w0_ref[...] + mm[:, :w_dim]
    soft = jnp.maximum(-w_lin, 0.0) + jnp.log(1.0 + jnp.exp(-jnp.abs(w_lin)))
    log_decay = -jnp.exp(-soft - 0.5)
    a = jax.nn.sigmoid(a0_ref[...] + mm[:, w_dim:2 * w_dim])
    g = mm[:, 2 * w_dim:]
    kk = k * kk_ref[...]
    kk = kk * lax.rsqrt(jnp.maximum(_head_sum(kk * kk, bd_ref[...]), 1e-24))
    r_o[...] = r
    k_o[...] = k * (1.0 + (a - 1.0) * ka_ref[...])
    v_o[...] = v
    lw_o[...] = log_decay
    an_o[...] = -kk
    bn_o[...] = kk * a
    g_o[...] = g


def _rwkv_prep(urw, seq, mu, w0, a0, k_k, k_a, wc, bd):
    t_dim = urw.shape[0]
    wc_hi, wc_lo = _split(wc)
    ts = min(256, seq)
    hb = ts // RWKV_HALO
    wide = pl.BlockSpec((ts, RWKV_PAD), lambda i: (i, 0))
    halo = pl.BlockSpec((RWKV_HALO, RWKV_PAD), lambda i: (jnp.maximum(i * hb - 1, 0), 0))
    vec = pl.BlockSpec((1, RWKV_WIDTH), lambda i: (0, 0))
    out = pl.BlockSpec((ts, RWKV_WIDTH), lambda i: (i, 0))
    return pl.pallas_call(
        functools.partial(_rwkv_prep_kernel, ts=ts, tiles_per_seq=seq // ts),
        grid=(t_dim // ts,),
        in_specs=[wide, halo, pl.BlockSpec((1, RWKV_PAD), lambda i: (0, 0)), vec, vec, vec, vec,
                  pl.BlockSpec((RWKV_LOW, 3 * RWKV_WIDTH), lambda i: (0, 0)),
                  pl.BlockSpec((RWKV_LOW, 3 * RWKV_WIDTH), lambda i: (0, 0)),
                  pl.BlockSpec((LANES, LANES), lambda i: (0, 0))],
        out_specs=[out] * 7,
        out_shape=[jax.ShapeDtypeStruct((t_dim, RWKV_WIDTH), F32)] * 7,
        compiler_params=_cparams(("arbitrary",), VMEM_BUDGET),
        name="rwkv_prep",
    )(urw, urw, mu, w0, a0, k_k, k_a, wc_hi, wc_lo, bd)


def _rwkv_chunk_kernel(r_ref, k_ref, v_ref, lw_ref, an_ref, bn_ref,
                       m_o, n_o, q_o, y_o, *, pairs):
    c_len = RWKV_CHUNK
    n2 = 2 * c_len
    ri = lax.broadcasted_iota(jnp.int32, (n2, n2), 0)
    ci = lax.broadcasted_iota(jnp.int32, (n2, n2), 1)
    strict = (ri % c_len) > (ci % c_len)
    incl = (ri % c_len) >= (ci % c_len)
    eye = (ri == ci).astype(F32)
    tri = (lax.broadcasted_iota(jnp.int32, (c_len, c_len), 0)
           >= lax.broadcasted_iota(jnp.int32, (c_len, c_len), 1)).astype(F32)
    lane = lax.broadcasted_iota(jnp.int32, (n2, LANES), 1)
    rowh = lax.broadcasted_iota(jnp.int32, (n2, LANES), 0)
    own = (lane < RWKV_HEAD_DIM) == (rowh < c_len)

    def stack(x):
        return jnp.where(own, jnp.concatenate([x, x], axis=0), 0.0)

    cum_all = _dot32(tri, lw_ref[...])
    every = range(pairs)
    sl = [slice(p * LANES, (p + 1) * LANES) for p in every]
    cum = [cum_all[:, s] for s in sl]
    tot = [c[c_len - 1:c_len, :] for c in cum]
    r_t = [stack(r_ref[:, sl[p]] * jnp.exp(cum[p])) for p in every]
    a_t = [stack(an_ref[:, sl[p]] * jnp.exp(cum[p] - lw_ref[:, sl[p]])).astype(BF16) for p in every]
    b_t = [stack(bn_ref[:, sl[p]] * jnp.exp(-cum[p])).astype(BF16) for p in every]
    k_t = [stack(k_ref[:, sl[p]] * jnp.exp(-cum[p])).astype(BF16) for p in every]
    b_e = [stack(bn_ref[:, sl[p]] * jnp.exp(tot[p] - cum[p])).astype(BF16) for p in every]
    k_e = [stack(k_ref[:, sl[p]] * jnp.exp(tot[p] - cum[p])).astype(BF16) for p in every]
    v_s = [stack(v_ref[:, sl[p]]).astype(BF16) for p in every]
    gram = [_dot(jnp.concatenate([a_t[p], r_t[p].astype(BF16)], axis=0),
                 jnp.concatenate([b_t[p], k_t[p]], axis=0), NT) for p in every]
    a_ab = [jnp.where(strict, g[:n2, :n2], 0.0) for g in gram]
    a_ak = [jnp.where(strict, g[:n2, n2:], 0.0).astype(BF16) for g in gram]
    r_ab = [jnp.where(incl, g[n2:, :n2], 0.0).astype(BF16) for g in gram]
    r_ak = [jnp.where(incl, g[n2:, n2:], 0.0).astype(BF16) for g in gram]
    inv = [eye + a for a in a_ab]
    pw = [a.astype(BF16) for a in a_ab]
    pw = [_dot(a, a) for a in pw]
    akv = [_dot(a_ak[p], v_s[p]).astype(BF16) for p in every]
    span = 2
    while 2 * span < c_len:
        pwb = [a.astype(BF16) for a in pw]
        both = [_dot(pwb[p], jnp.concatenate([pwb[p], inv[p].astype(BF16)], axis=1)) for p in every]
        pw = [b[:, :n2] for b in both]
        inv = [inv[p] + both[p][:, n2:] for p in every]
        span *= 2
    inv = [inv[p] + _dot(pw[p].astype(BF16), inv[p].astype(BF16)) for p in every]
    wx = [_dot(inv[p].astype(BF16), jnp.concatenate([a_t[p], akv[p]], axis=1)).astype(BF16)
          for p in every]
    qy = [_dot(r_ab[p], wx[p]) for p in every]
    rv = [_dot(r_ak[p], v_s[p]) for p in every]
    mn = [_dot(b_e[p], wx[p], TN) for p in every]
    kv = [_dot(k_e[p], v_s[p], TN) for p in every]
    for p in every:
        q_s = r_t[p] + qy[p][:, :n2]
        y_s = qy[p][:, n2:] + rv[p]
        m_o[0, p] = eye * jnp.exp(tot[p]) + mn[p][:, :n2]
        n_o[0, p] = mn[p][:, n2:] + kv[p]
        q_o[:, sl[p]] = q_s[:c_len] + q_s[c_len:]
        y_o[:, sl[p]] = y_s[:c_len] + y_s[c_len:]


def _rwkv_chunks(r, k, v, lw, an, bn):
    t_dim = r.shape[0]
    pairs = RWKV_PAIRS
    n_chunks = t_dim // RWKV_CHUNK
    tile = pl.BlockSpec((RWKV_CHUNK, pairs * LANES), lambda c, p: (c, p))
    op = pl.BlockSpec((1, pairs, LANES, LANES), lambda c, p: (c, p, 0, 0))
    op_shape = jax.ShapeDtypeStruct((n_chunks, RWKV_PAIRS, LANES, LANES), F32)
    row_shape = jax.ShapeDtypeStruct((t_dim, RWKV_WIDTH), F32)
    return pl.pallas_call(
        functools.partial(_rwkv_chunk_kernel, pairs=pairs),
        grid=(n_chunks, RWKV_PAIRS // pairs),
        in_specs=[tile] * 6,
        out_specs=[op, op, tile, tile],
        out_shape=[op_shape, op_shape, row_shape, row_shape],
        compiler_params=_cparams(("arbitrary", "arbitrary")),
        name="rwkv_chunk",
    )(r, k, v, lw, an, bn)


def _rwkv_sweep_kernel(m_ref, n_ref, q_ref, y0_ref, r_ref, k_ref, v_ref, g_ref,
                       rk_ref, lg_ref, lb_ref, bd_ref, o_ref, z_ref, y_ref, *, pairs, chunks):
    @pl.when(pl.program_id(1) == 0)
    def _():
        z_ref[...] = jnp.zeros_like(z_ref)

    sl = [slice(p * LANES, (p + 1) * LANES) for p in range(pairs)]
    z = [z_ref[p] for p in range(pairs)]
    for c in range(chunks):
        rows = slice(c * RWKV_CHUNK, (c + 1) * RWKV_CHUNK)
        zs = [_split(zp) for zp in z]
        ys = [_dot3s(q_ref[rows, sl[p]], zs[p]) for p in range(pairs)]
        zn = [_dot3s(m_ref[c, p], zs[p]) for p in range(pairs)]
        for p in range(pairs):
            y_ref[rows, sl[p]] = ys[p] + y0_ref[rows, sl[p]]
        z = [zn[p] + n_ref[c, p] for p in range(pairs)]
    for p in range(pairs):
        z_ref[p] = z[p]

    bd = bd_ref[...]
    y = y_ref[...]
    mu = _head_sum(y, bd) * (1.0 / RWKV_HEAD_DIM)
    yc = y - mu
    var = _head_sum(yc * yc, bd) * (1.0 / RWKV_HEAD_DIM)
    yn = yc * lax.rsqrt(var + RWKV_GN_EPS) * lg_ref[...] + lb_ref[...]
    bonus = _head_sum(r_ref[...] * k_ref[...] * rk_ref[...], bd) * v_ref[...]
    o_ref[...] = ((yn + bonus) * g_ref[...]).astype(o_ref.dtype)


def _rwkv_sweep(m_op, n_op, q_op, y0, r, k, v, g, r_k, ln_g, ln_b, bd, batch, seq):
    t_dim = q_op.shape[0]
    pairs = RWKV_PAIRS
    chunks = min(4, seq // RWKV_CHUNK)
    steps = seq // (RWKV_CHUNK * chunks)
    rows = RWKV_CHUNK * chunks
    tile = pl.BlockSpec((rows, RWKV_WIDTH), lambda b, s: (b * steps + s, 0))
    op = pl.BlockSpec((chunks, pairs, LANES, LANES), lambda b, s: (b * steps + s, 0, 0, 0))
    vec = pl.BlockSpec((1, RWKV_WIDTH), lambda b, s: (0, 0))
    return pl.pallas_call(
        functools.partial(_rwkv_sweep_kernel, pairs=pairs, chunks=chunks),
        grid=(batch, steps),
        in_specs=[op, op] + [tile] * 6 + [vec] * 3 + [pl.BlockSpec((LANES, LANES), lambda b, s: (0, 0))],
        out_specs=tile,
        out_shape=jax.ShapeDtypeStruct((t_dim, RWKV_WIDTH), BF16),
        scratch_shapes=[pltpu.VMEM((pairs, LANES, LANES), F32), pltpu.VMEM((rows, RWKV_WIDTH), F32)],
        compiler_params=_cparams(("arbitrary", "arbitrary"), VMEM_BUDGET),
        name="rwkv_sweep",
    )(m_op, n_op, q_op, y0, r, k, v, g, r_k, ln_g, ln_b, bd)


def _merge_kernel(yp_ref, ya_ref, yr_ref, wp_ref, wa_ref, wr_ref, g0_ref, g1_ref, g2_ref,
                  b0_ref, b1_ref, b2_ref, o_ref, wpb, wab, wrb):
    @pl.when(pl.program_id(1) == 0)
    def _():
        wpb[...] = wp_ref[...].astype(BF16)
        wab[...] = wa_ref[...].astype(BF16)
        wrb[...] = wr_ref[...].astype(BF16)

    m = jax.nn.sigmoid(g0_ref[...] + b0_ref[...]) * _dot(yp_ref[...], wpb[...])
    m = m + jax.nn.sigmoid(g1_ref[...] + b1_ref[...]) * _dot(ya_ref[...], wab[...])
    m = m + jax.nn.sigmoid(g2_ref[...] + b2_ref[...]) * _dot(yr_ref[...], wrb[...])
    o_ref[...] = m.astype(o_ref.dtype)


def _merge(y_pool, y_attn, y_rwkv, w_pool, w_attn, w_rwkv, layer, gate_logits, b_gate):
    t_dim = y_pool.shape[0]
    tm = min(1024, t_dim)
    tn = 512
    nb = D_MODEL // tn
    act = lambda width: pl.BlockSpec((tm, width), lambda j, i: (i, 0))
    wgt = lambda width: pl.BlockSpec((None, width, tn), lambda j, i: (layer, 0, j))
    gate = lambda br: pl.BlockSpec((tm, tn), lambda j, i: (i, br * nb + j))
    bias = lambda br: pl.BlockSpec((1, tn), lambda j, i: (0, br * nb + j))
    return pl.pallas_call(
        _merge_kernel, grid=(nb, t_dim // tm),
        in_specs=[act(POOL_WIDTH), act(SB_WIDTH), act(RWKV_WIDTH),
                  wgt(POOL_WIDTH), wgt(SB_WIDTH), wgt(RWKV_WIDTH),
                  gate(0), gate(1), gate(2), bias(0), bias(1), bias(2)],
        out_specs=pl.BlockSpec((tm, tn), lambda j, i: (i, j)),
        out_shape=jax.ShapeDtypeStruct((t_dim, D_MODEL), BF16),
        scratch_shapes=[pltpu.VMEM((POOL_WIDTH, tn), BF16), pltpu.VMEM((SB_WIDTH, tn), BF16),
                        pltpu.VMEM((RWKV_WIDTH, tn), BF16)],
        compiler_params=_cparams(("arbitrary", "arbitrary"), VMEM_BUDGET),
        name="merge",
    )(y_pool, y_attn, y_rwkv, w_pool, w_attn, w_rwkv, gate_logits, gate_logits, gate_logits,
      b_gate, b_gate, b_gate)


def _ffn_up_kernel(a_ref, wa_ref, wl_ref, cw_ref, cb_ref, o_ref, wb_ref, halo_ref, *, tiles_per_seq):
    i = pl.program_id(1)
    tm, tn = o_ref.shape

    @pl.when(i == 0)
    def _():
        _cast_panel(wa_ref, wb_ref.at[:, :tn])
        _cast_panel(wl_ref, wb_ref.at[:, tn:])
        halo_ref[...] = jnp.zeros_like(halo_ref)

    prev = jnp.where(i % tiles_per_seq == 0, 0.0, halo_ref[...])
    row = lax.broadcasted_iota(jnp.int32, (DOT_ROWS, 1), 0)
    for c in range(tm // DOT_ROWS):
        rows = slice(c * DOT_ROWS, (c + 1) * DOT_ROWS)
        res = _dot(a_ref[rows, :], wb_ref[...])
        act, lin = res[:, :tn], res[:, tn:]
        p1 = jnp.where(row == 0, prev[7:8, :], pltpu.roll(act, 1, axis=0))
        p2 = jnp.where(row == 0, prev[6:7, :],
                       jnp.where(row == 1, prev[7:8, :], pltpu.roll(act, 2, axis=0)))
        cv = p2 * cw_ref[0:1, :] + p1 * cw_ref[1:2, :] + act * cw_ref[2:3, :] + cb_ref[...]
        gelu = 0.5 * cv * (1.0 + lax.erf(cv * (2.0 ** -0.5)))
        o_ref[rows, :] = (gelu * lin).astype(o_ref.dtype)
        prev = act[DOT_ROWS - 8:, :]
    halo_ref[...] = prev


def _ffn_up(a, w_up, layer, conv_w, conv_b, seq):
    t_dim, k_dim = a.shape
    tm = min(2048, seq)
    tn = 256
    nb = D_FF // tn
    return pl.pallas_call(
        functools.partial(_ffn_up_kernel, tiles_per_seq=seq // tm),
        grid=(nb, t_dim // tm),
        in_specs=[pl.BlockSpec((tm, k_dim), lambda j, i: (i, 0)),
                  pl.BlockSpec((None, k_dim, tn), lambda j, i: (layer, 0, j)),
                  pl.BlockSpec((None, k_dim, tn), lambda j, i: (layer, 0, nb + j)),
                  pl.BlockSpec((3, tn), lambda j, i: (0, j)),
                  pl.BlockSpec((1, tn), lambda j, i: (0, j))],
        out_specs=pl.BlockSpec((tm, tn), lambda j, i: (i, j)),
        out_shape=jax.ShapeDtypeStruct((t_dim, D_FF), BF16),
        scratch_shapes=[pltpu.VMEM((k_dim, 2 * tn), BF16), pltpu.VMEM((8, tn), F32)],
        compiler_params=_cparams(("arbitrary", "arbitrary"), VMEM_BUDGET),
        name="ffn_up",
    )(a, w_up, w_up, conv_w, conv_b.reshape(1, D_FF))


def kernel(x, w_in, b_gate, pool_w, pool_scale, rwkv_mu, rwkv_w0, rwkv_w2, rwkv_a0, rwkv_a2, rwkv_g2, rwkv_k_k, rwkv_k_a, rwkv_r_k, rwkv_ln_g, rwkv_ln_b, w_branch_pool, w_branch_attn, w_branch_rwkv, w_out, ln1_g, ln1_b, w_up, ffn_conv_w, ffn_conv_b, w_down, ln2_g, ln2_b):
    batch, seq, _ = x.shape
    t_dim = batch * seq
    xf = x.reshape(t_dim, D_MODEL)
    xb = xf.astype(BF16)
    half = lax.broadcasted_iota(jnp.int32, (LANES, LANES), 0) // RWKV_HEAD_DIM
    bd = (half == half.T).astype(BF16)
    vec = lambda p: p.reshape(1, -1)
    col = lax.broadcasted_iota(jnp.int32, (1, N_MAIN), 1)
    q_cols = (col >= POOL_WIDTH) & (col < POOL_WIDTH + SB_WIDTH)
    main_scale = jnp.where(q_cols, SB_HEAD_DIM ** -0.5 * math.log2(math.e), 1.0).astype(F32)
    w_in_t = jnp.swapaxes(w_in, 1, 2)
    for l in range(DEPTH):
        main = _mm_t(xb, w_in_t, l, row0=0, n_cols=N_MAIN, out_dtype=BF16, tm=2048,
                     col_scale=main_scale)
        urw = _mm_t(xb, w_in_t, l, row0=N_MAIN, n_cols=RWKV_PAD, out_dtype=BF16, tm=2048)
        gate_logits = _mm_t(xb, w_in_t, l, row0=GATE_OFF, n_cols=3 * D_MODEL, out_dtype=BF16,
                            tm=2048)

        y_pool = _pool(main, pool_w[l], pool_scale[l], seq)
        y_attn = _attention(main, batch, seq)

        wc = jnp.zeros((RWKV_LOW, 3 * RWKV_WIDTH), F32)
        wc = wc.at[:RWKV_DECAY_RANK, :RWKV_WIDTH].set(rwkv_w2[l])
        wc = wc.at[RWKV_DECAY_RANK:RWKV_DECAY_RANK + RWKV_A_RANK, RWKV_WIDTH:2 * RWKV_WIDTH].set(rwkv_a2[l])
        lo = RWKV_DECAY_RANK + RWKV_A_RANK
        wc = wc.at[lo:lo + RWKV_GATE_RANK, 2 * RWKV_WIDTH:].set(rwkv_g2[l])
        mu = jnp.pad(rwkv_mu[l], (0, RWKV_PAD - RWKV_SHIFT_WIDTH)).reshape(1, RWKV_PAD)
        r, k, v, lw, an, bn, g = _rwkv_prep(urw, seq, mu, vec(rwkv_w0[l]), vec(rwkv_a0[l]),
                                            vec(rwkv_k_k[l]), vec(rwkv_k_a[l]), wc, bd)
        m_op, n_op, q_op, y0 = _rwkv_chunks(r, k, v, lw, an, bn)
        y_rwkv = _rwkv_sweep(m_op, n_op, q_op, y0, r, k, v, g, vec(rwkv_r_k[l]),
                             vec(rwkv_ln_g[l]), vec(rwkv_ln_b[l]), bd, batch, seq)

        merged = _merge(y_pool, y_attn, y_rwkv, w_branch_pool, w_branch_attn, w_branch_rwkv, l,
                        gate_logits, vec(b_gate[l]))
        sub = _mm(merged, w_out, l, out_dtype=BF16, tm=2048, tn=512)
        xf, xb = _ln(xf, sub, ln1_g[l], ln1_b[l], with_bf16=True)

        h = _ffn_up(xb, w_up, l, ffn_conv_w[l], ffn_conv_b[l], seq)
        ffn = _mm(h, w_down, l, out_dtype=BF16, tm=512, tn=512, single_buffer_w=True)
        if l + 1 < DEPTH:
            xf, xb = _ln(xf, ffn, ln2_g[l], ln2_b[l], with_bf16=True)
        else:
            xf, = _ln(xf, ffn, ln2_g[l], ln2_b[l], with_bf16=False)
    return xf.reshape(batch, seq, D_MODEL)
```

```python
import functools
import math

import jax
import jax.numpy as jnp
from jax import lax
from jax.experimental import pallas as pl
from jax.experimental.pallas import tpu as pltpu

F32 = jnp.float32
BF16 = jnp.bfloat16

D_MODEL = 4096
DEPTH = 2
POOL_WIDTH = 1024
POOL_WINDOWS = (2, 4, 8, 16)
POOL_GROUP_DIM = 256
POOL_HALO = 16
RWKV_HALO = 16
SB_HEAD_DIM = 128
SB_WIDTH = 2048
SB_HEADS = 16
RWKV_WIDTH = 1024
RWKV_HEAD_DIM = 64
RWKV_DECAY_RANK = 64
RWKV_A_RANK = 64
RWKV_GATE_RANK = 160
RWKV_SHIFT_WIDTH = 3 * RWKV_WIDTH + RWKV_DECAY_RANK + RWKV_A_RANK + RWKV_GATE_RANK
RWKV_GN_EPS = 64e-5
N_MAIN = POOL_WIDTH + 3 * SB_WIDTH
RWKV_PAD = 3584
RWKV_LOW = 384
GATE_OFF = N_MAIN + RWKV_SHIFT_WIDTH
D_FF = 11008
DEEPNORM_ALPHA = (2 * DEPTH) ** 0.25
LN_EPS = 1e-5

LANES = 128
RWKV_CHUNK = 64
RWKV_PAIRS = RWKV_WIDTH // LANES
VMEM_BUDGET = 58 * 1024 * 1024
CAST_ROWS = 256
DOT_ROWS = 512
DOT_ROWS_T = 512


def _cparams(sem, vmem=None):
    return pltpu.CompilerParams(dimension_semantics=sem, vmem_limit_bytes=vmem)


def _dot(a, b, dims=None, prec=None):
    if dims is None:
        dims = (((a.ndim - 1,), (0,)), ((), ()))
    return lax.dot_general(a, b, dims, precision=prec, preferred_element_type=F32)


def _dot32(a, b, dims=None):
    return _dot(a, b, dims, prec=lax.Precision.HIGHEST)


def _split(a):
    hi = a.astype(BF16)
    return hi, (a - hi.astype(F32)).astype(BF16)


def _dot3s(a, b_split):
    ah, al = _split(a)
    bh, bl = b_split
    return _dot(ah, bh) + _dot(ah, bl) + _dot(al, bh)


NT = (((1,), (1,)), ((), ()))
TN = (((0,), (0,)), ((), ()))


def _cast_panel(w_ref, wb_ref):
    def cast(c, carry):
        r = pl.multiple_of(c * CAST_ROWS, CAST_ROWS)
        wb_ref[pl.ds(r, CAST_ROWS), :] = w_ref[pl.ds(r, CAST_ROWS), :].astype(BF16)
        return carry
    lax.fori_loop(0, wb_ref.shape[0] // CAST_ROWS, cast, 0)


def _mm_kernel(a_ref, w_ref, o_ref, wb_ref):
    tm = o_ref.shape[0]

    @pl.when(pl.program_id(1) == 0)
    def _():
        _cast_panel(w_ref, wb_ref)

    for c in range(tm // DOT_ROWS):
        rows = slice(c * DOT_ROWS, (c + 1) * DOT_ROWS)
        o_ref[rows, :] = _dot(a_ref[rows, :], wb_ref[...]).astype(o_ref.dtype)


def _mm(a, w, layer, *, out_dtype, tm, tn, single_buffer_w=False):
    m_dim, k_dim = a.shape
    n_cols = w.shape[2]
    tm = min(tm, m_dim)
    w_mode = dict(pipeline_mode=pl.Buffered(1)) if single_buffer_w else {}
    return pl.pallas_call(
        _mm_kernel,
        grid=(n_cols // tn, m_dim // tm),
        in_specs=[pl.BlockSpec((tm, k_dim), lambda j, i: (i, 0)),
                  pl.BlockSpec((None, k_dim, tn), lambda j, i: (layer, 0, j), **w_mode)],
        out_specs=pl.BlockSpec((tm, tn), lambda j, i: (i, j)),
        out_shape=jax.ShapeDtypeStruct((m_dim, n_cols), out_dtype),
        scratch_shapes=[pltpu.VMEM((k_dim, tn), BF16)],
        compiler_params=_cparams(("arbitrary", "arbitrary"), VMEM_BUDGET),
        name="mm",
    )(a, w)


def _mm_pf_kernel(a_ref, w_hbm, o_ref, stage_ref, wb_ref, sem, *, layer, n_panels):
    j, i = pl.program_id(0), pl.program_id(1)
    tm, tn = o_ref.shape

    def panel_copy(p):
        cols = pl.ds(pl.multiple_of(p * tn, tn), tn)
        return pltpu.make_async_copy(w_hbm.at[layer, :, cols], stage_ref, sem)

    @pl.when(i == 0)
    def _():
        @pl.when(j == 0)
        def _():
            panel_copy(0).start()
        panel_copy(j).wait()
        _cast_panel(stage_ref, wb_ref)

        @pl.when(j + 1 < n_panels)
        def _():
            panel_copy(j + 1).start()

    for c in range(tm // DOT_ROWS):
        rows = slice(c * DOT_ROWS, (c + 1) * DOT_ROWS)
        o_ref[rows, :] = _dot(a_ref[rows, :], wb_ref[...]).astype(o_ref.dtype)


def _mm_prefetch(a, w, layer, *, out_dtype, tm, tn):
    m_dim, k_dim = a.shape
    n_cols = w.shape[2]
    tm = min(tm, m_dim)
    return pl.pallas_call(
        functools.partial(_mm_pf_kernel, layer=layer, n_panels=n_cols // tn),
        grid=(n_cols // tn, m_dim // tm),
        in_specs=[pl.BlockSpec((tm, k_dim), lambda j, i: (i, 0)),
                  pl.BlockSpec(memory_space=pl.ANY)],
        out_specs=pl.BlockSpec((tm, tn), lambda j, i: (i, j)),
        out_shape=jax.ShapeDtypeStruct((m_dim, n_cols), out_dtype),
        scratch_shapes=[pltpu.VMEM((k_dim, tn), F32), pltpu.VMEM((k_dim, tn), BF16),
                        pltpu.SemaphoreType.DMA(())],
        compiler_params=_cparams(("arbitrary", "arbitrary"), VMEM_BUDGET),
        name="mm_prefetch",
    )(a, w)


def _mm_t_kernel(*refs, scaled):
    a_ref, w_ref = refs[0], refs[1]
    s_ref = refs[2] if scaled else None
    o_ref, wb_ref = refs[-2], refs[-1]
    tm, tn = o_ref.shape

    @pl.when(pl.program_id(1) == 0)
    def _():
        _cast_panel(w_ref, wb_ref)

    for c in range(tm // DOT_ROWS_T):
        rows = slice(c * DOT_ROWS_T, (c + 1) * DOT_ROWS_T)
        res = _dot(a_ref[rows, :], wb_ref[...], NT)
        if scaled:
            res = res * s_ref[...]
        o_ref[rows, :] = res.astype(o_ref.dtype)


def _mm_t(a, wt, layer, *, row0, n_cols, out_dtype, tm=1024, tn=512, col_scale=None):
    m_dim, k_dim = a.shape
    tm = min(tm, m_dim)
    operands = [a, wt]
    in_specs = [pl.BlockSpec((tm, k_dim), lambda j, i: (i, 0)),
                pl.BlockSpec((None, pl.Element(tn), pl.Element(k_dim)),
                             lambda j, i: (layer, pl.multiple_of(row0 + j * tn, 32), 0))]
    if col_scale is not None:
        operands.append(col_scale)
        in_specs.append(pl.BlockSpec((1, tn), lambda j, i: (0, j)))
    return pl.pallas_call(
        functools.partial(_mm_t_kernel, scaled=col_scale is not None),
        grid=(n_cols // tn, m_dim // tm),
        in_specs=in_specs,
        out_specs=pl.BlockSpec((tm, tn), lambda j, i: (i, j)),
        out_shape=jax.ShapeDtypeStruct((m_dim, n_cols), out_dtype),
        scratch_shapes=[pltpu.VMEM((tn, k_dim), BF16)],
        compiler_params=_cparams(("arbitrary", "arbitrary"), VMEM_BUDGET),
        name="mm_t",
    )(*operands)


def _ln_kernel(x_ref, y_ref, g_ref, b_ref, of_ref, *maybe_ob_ref):
    z = DEEPNORM_ALPHA * x_ref[...] + y_ref[...]
    mu = jnp.mean(z, axis=-1, keepdims=True)
    zc = z - mu
    var = jnp.mean(zc * zc, axis=-1, keepdims=True)
    out = zc * lax.rsqrt(var + LN_EPS) * g_ref[...] + b_ref[...]
    of_ref[...] = out
    for ob_ref in maybe_ob_ref:
        ob_ref[...] = out.astype(BF16)


def _ln(x, y, g, b, *, with_bf16):
    t_dim = x.shape[0]
    tm = min(256, t_dim)
    row = pl.BlockSpec((tm, D_MODEL), lambda i: (i, 0))
    vec = pl.BlockSpec((1, D_MODEL), lambda i: (0, 0))
    out_shape = [jax.ShapeDtypeStruct((t_dim, D_MODEL), F32)]
    if with_bf16:
        out_shape.append(jax.ShapeDtypeStruct((t_dim, D_MODEL), BF16))
    return pl.pallas_call(
        _ln_kernel, grid=(t_dim // tm,),
        in_specs=[row, row, vec, vec], out_specs=[row] * len(out_shape),
        out_shape=out_shape,
        compiler_params=_cparams(("arbitrary",), VMEM_BUDGET),
        name="ln",
    )(x, y, g.reshape(1, D_MODEL), b.reshape(1, D_MODEL))


def _pool_kernel(u_ref, h_ref, pw_ref, ps_ref, o_ref, *, ts, tiles_per_seq):
    it = pl.program_id(0) % tiles_per_seq
    x = u_ref[...].astype(F32)
    halo = jnp.where(it == 0, 0.0, h_ref[...].astype(F32))
    xe = jnp.concatenate([halo, x], axis=0)
    t_seq = lax.broadcasted_iota(jnp.int32, (ts, 1), 0) + it * ts
    for g, win in enumerate(POOL_WINDOWS):
        cols = slice(g * POOL_GROUP_DIM, (g + 1) * POOL_GROUP_DIM)
        s = xe[:, cols]
        span = 1
        while span < win:
            s = s + pltpu.roll(s, span, axis=0)
            span *= 2
        cnt = jnp.minimum(t_seq + 1, win).astype(F32)
        d = s[POOL_HALO:, :] / cnt - x[:, cols]
        y = _dot(d.astype(BF16), pw_ref[g].astype(BF16))
        o_ref[:, cols] = (y * ps_ref[:, cols]).astype(o_ref.dtype)


def _pool(main, pool_w, pool_scale, seq):
    t_dim = main.shape[0]
    ts = min(512, seq)
    hb = ts // POOL_HALO
    return pl.pallas_call(
        functools.partial(_pool_kernel, ts=ts, tiles_per_seq=seq // ts),
        grid=(t_dim // ts,),
        in_specs=[pl.BlockSpec((ts, POOL_WIDTH), lambda i: (i, 0)),
                  pl.BlockSpec((POOL_HALO, POOL_WIDTH), lambda i: (jnp.maximum(i * hb - 1, 0), 0)),
                  pl.BlockSpec((4, POOL_GROUP_DIM, POOL_GROUP_DIM), lambda i: (0, 0, 0)),
                  pl.BlockSpec((1, POOL_WIDTH), lambda i: (0, 0))],
        out_specs=pl.BlockSpec((ts, POOL_WIDTH), lambda i: (i, 0)),
        out_shape=jax.ShapeDtypeStruct((t_dim, POOL_WIDTH), BF16),
        compiler_params=_cparams(("arbitrary",)),
        name="pool",
    )(main, main, pool_w, pool_scale.reshape(1, POOL_WIDTH))


def _attn_kernel(q_ref, k_ref, v_ref, r_ref, o_ref, *, tq, tk):
    iq = pl.program_id(2)
    rmat = r_ref[...]

    def front(items):
        starts = [pl.multiple_of(j * tk, tk) for _, _, j, _ in items]
        z = [_dot(q, k_ref[pl.ds(r0, tk), :], NT) for (q, _, _, _), r0 in zip(items, starts)]
        nz = [-zj for zj in z]
        lk = [jnp.minimum(m, 0.0) - jnp.log2(1.0 + jnp.exp2(jnp.minimum(zj, m)))
              for zj, m in zip(z, nz)]
        causal = []
        for b, (q, row0, j, masked) in enumerate(items):
            causal.append(None)
            if masked:
                shape = (q.shape[0], tk)
                causal[b] = (lax.broadcasted_iota(jnp.int32, shape, 1) + j * tk
                             < lax.broadcasted_iota(jnp.int32, shape, 0) + row0)
                lk[b] = jnp.where(causal[b], lk[b], 0.0)
        later = [_dot(x.astype(BF16), rmat) for x in lk]
        base = [zj + lt for zj, lt in zip(z, later)]
        return list(zip(base, lk, causal, starts))

    def back(done, acc, run):
        for base, lk, causal, r0 in done:
            att = jnp.exp2(base + run)
            if causal is not None:
                att = jnp.where(causal, att, 0.0)
            acc = acc + _dot(att.astype(BF16), v_ref[pl.ds(r0, tk), :])
            run = run + jnp.sum(lk, axis=1, keepdims=True)
        return acc, run

    zero_acc = jnp.zeros((tk, SB_HEAD_DIM), F32)
    zero_run = jnp.zeros((tk, 1), F32)
    ratio = tq // tk
    items = []
    for r in range(ratio):
        q_r, row_r = q_ref[r * tk:(r + 1) * tk, :], iq * tq + r * tk
        items += [(q_r, row_r, ratio * iq + s, s == r) for s in range(r, -1, -1)]
    diag = front(items)
    parts, first = [], 0
    for r in range(ratio):
        parts.append(back(diag[first:first + r + 1], zero_acc, zero_run))
        first += r + 1
    acc = jnp.concatenate([p[0] for p in parts], axis=0)
    run = jnp.concatenate([p[1] for p in parts], axis=0)
    q = q_ref[...]

    def pair(n, carry):
        j = ratio * iq - 1 - 2 * n
        return back(front([(q, 0, j, False), (q, 0, j - 1, False)]), *carry)

    acc, run = lax.fori_loop(0, ratio * iq // 2, pair, (acc, run))
    o_ref[...] = acc.astype(o_ref.dtype)


def _attention(main, batch, seq):
    t_dim = main.shape[0]
    tk = min(256, seq // 2)
    tq = 2 * tk
    nq = seq // tq
    q_blk0 = POOL_WIDTH // LANES
    k_blk0 = q_blk0 + SB_HEADS
    v_blk0 = k_blk0 + SB_HEADS
    rmat = (lax.broadcasted_iota(jnp.int32, (tk, tk), 0)
            >= lax.broadcasted_iota(jnp.int32, (tk, tk), 1)).astype(BF16)
    return pl.pallas_call(
        functools.partial(_attn_kernel, tq=tq, tk=tk),
        grid=(batch, SB_HEADS, nq),
        in_specs=[pl.BlockSpec((tq, SB_HEAD_DIM), lambda b, h, i: (b * nq + i, q_blk0 + h)),
                  pl.BlockSpec((seq, SB_HEAD_DIM), lambda b, h, i: (b, k_blk0 + h)),
                  pl.BlockSpec((seq, SB_HEAD_DIM), lambda b, h, i: (b, v_blk0 + h)),
                  pl.BlockSpec((tk, tk), lambda b, h, i: (0, 0))],
        out_specs=pl.BlockSpec((tq, SB_HEAD_DIM), lambda b, h, i: (b * nq + i, h)),
        out_shape=jax.ShapeDtypeStruct((t_dim, SB_WIDTH), BF16),
        compiler_params=_cparams(("arbitrary", "arbitrary", "arbitrary")),
        name="attn",
    )(main, main, main, rmat)


def _head_sum(x, bd):
    hi, lo = _split(x)
    return jnp.concatenate(
        [_dot(hi[:, c * LANES:(c + 1) * LANES], bd) + _dot(lo[:, c * LANES:(c + 1) * LANES], bd)
         for c in range(x.shape[1] // LANES)], axis=1)


def _rwkv_prep_kernel(u_ref, h_ref, mu_ref, w0_ref, a0_ref, kk_ref, ka_ref, wch_ref, wcl_ref, bd_ref,
                      r_o, k_o, v_o, lw_o, an_o, bn_o, g_o, *, ts, tiles_per_seq):
    it = pl.program_id(0) % tiles_per_seq
    u = u_ref[...].astype(F32)
    last = jnp.where(it == 0, 0.0, h_ref[RWKV_HALO - 1:RWKV_HALO, :].astype(F32))
    row = lax.broadcasted_iota(jnp.int32, (ts, 1), 0)
    prev = jnp.where(row == 0, last, pltpu.roll(u, 1, axis=0))
    x = u + (prev - u) * mu_ref[...]
    w_dim = RWKV_WIDTH
    r, k, v = x[:, :w_dim], x[:, w_dim:2 * w_dim], x[:, 2 * w_dim:3 * w_dim]
    low = x[:, 3 * w_dim:3 * w_dim + RWKV_LOW]
    lane = lax.broadcasted_iota(jnp.int32, (1, RWKV_LOW), 1)
    low = jnp.where(lane < RWKV_DECAY_RANK, jnp.tanh(low),
                    jnp.where(lane < RWKV_DECAY_RANK + RWKV_A_RANK, low, jax.nn.sigmoid(low)))
    mm = _dot3s(low, (wch_ref[...], wcl_ref[...]))
    w_lin = w0_ref[...] + mm[:, :w_dim]
    soft = jnp.maximum(-w_lin, 0.0) + jnp.log(1.0 + jnp.exp(-jnp.abs(w_lin)))
    log_decay = -jnp.exp(-soft - 0.5)
    a = jax.nn.sigmoid(a0_ref[...] + mm[:, w_dim:2 * w_dim])
    g = mm[:, 2 * w_dim:]
    kk = k * kk_ref[...]
    kk = kk * lax.rsqrt(jnp.maximum(_head_sum(kk * kk, bd_ref[...]), 1e-24))
    r_o[...] = r
    k_o[...] = k * (1.0 + (a - 1.0) * ka_ref[...])
    v_o[...] = v
    lw_o[...] = log_decay
    an_o[...] = -kk
    bn_o[...] = kk * a
    g_o[...] = g


def _rwkv_prep(urw, seq, mu, w0, a0, k_k, k_a, wc, bd):
    t_dim = urw.shape[0]
    wc_hi, wc_lo = _split(wc)
    ts = min(256, seq)
    hb = ts // RWKV_HALO
    wide = pl.BlockSpec((ts, RWKV_PAD), lambda i: (i, 0))
    halo = pl.BlockSpec((RWKV_HALO, RWKV_PAD), lambda i: (jnp.maximum(i * hb - 1, 0), 0))
    vec = pl.BlockSpec((1, RWKV_WIDTH), lambda i: (0, 0))
    out = pl.BlockSpec((ts, RWKV_WIDTH), lambda i: (i, 0))
    return pl.pallas_call(
        functools.partial(_rwkv_prep_kernel, ts=ts, tiles_per_seq=seq // ts),
        grid=(t_dim // ts,),
        in_specs=[wide, halo, pl.BlockSpec((1, RWKV_PAD), lambda i: (0, 0)), vec, vec, vec, vec,
                  pl.BlockSpec((RWKV_LOW, 3 * RWKV_WIDTH), lambda i: (0, 0)),
                  pl.BlockSpec((RWKV_LOW, 3 * RWKV_WIDTH), lambda i: (0, 0)),
                  pl.BlockSpec((LANES, LANES), lambda i: (0, 0))],
        out_specs=[out] * 7,
        out_shape=[jax.ShapeDtypeStruct((t_dim, RWKV_WIDTH), F32)] * 7,
        compiler_params=_cparams(("arbitrary",), VMEM_BUDGET),
        name="rwkv_prep",
    )(urw, urw, mu, w0, a0, k_k, k_a, wc_hi, wc_lo, bd)


def _rwkv_chunk_kernel(r_ref, k_ref, v_ref, lw_ref, an_ref, bn_ref,
                       m_o, n_o, q_o, y_o, *, pairs):
    c_len = RWKV_CHUNK
    n2 = 2 * c_len
    ri = lax.broadcasted_iota(jnp.int32, (n2, n2), 0)
    ci = lax.broadcasted_iota(jnp.int32, (n2, n2), 1)
    strict = (ri % c_len) > (ci % c_len)
    incl = (ri % c_len) >= (ci % c_len)
    eye = (ri == ci).astype(F32)
    tri = (lax.broadcasted_iota(jnp.int32, (c_len, c_len), 0)
           >= lax.broadcasted_iota(jnp.int32, (c_len, c_len), 1)).astype(F32)
    lane = lax.broadcasted_iota(jnp.int32, (n2, LANES), 1)
    rowh = lax.broadcasted_iota(jnp.int32, (n2, LANES), 0)
    own = (lane < RWKV_HEAD_DIM) == (rowh < c_len)

    def stack(x):
        return jnp.where(own, jnp.concatenate([x, x], axis=0), 0.0)

    cum_all = _dot32(tri, lw_ref[...])
    every = range(pairs)
    sl = [slice(p * LANES, (p + 1) * LANES) for p in every]
    cum = [cum_all[:, s] for s in sl]
    tot = [c[c_len - 1:c_len, :] for c in cum]
    r_t = [stack(r_ref[:, sl[p]] * jnp.exp(cum[p])) for p in every]
    a_t = [stack(an_ref[:, sl[p]] * jnp.exp(cum[p] - lw_ref[:, sl[p]])).astype(BF16) for p in every]
    b_t = [stack(bn_ref[:, sl[p]] * jnp.exp(-cum[p])).astype(BF16) for p in every]
    k_t = [stack(k_ref[:, sl[p]] * jnp.exp(-cum[p])).astype(BF16) for p in every]
    b_e = [stack(bn_ref[:, sl[p]] * jnp.exp(tot[p] - cum[p])).astype(BF16) for p in every]
    k_e = [stack(k_ref[:, sl[p]] * jnp.exp(tot[p] - cum[p])).astype(BF16) for p in every]
    v_s = [stack(v_ref[:, sl[p]]).astype(BF16) for p in every]
    gram = [_dot(jnp.concatenate([a_t[p], r_t[p].astype(BF16)], axis=0),
                 jnp.concatenate([b_t[p], k_t[p]], axis=0), NT) for p in every]
    a_ab = [jnp.where(strict, g[:n2, :n2], 0.0) for g in gram]
    a_ak = [jnp.where(strict, g[:n2, n2:], 0.0).astype(BF16) for g in gram]
    r_ab = [jnp.where(incl, g[n2:, :n2], 0.0).astype(BF16) for g in gram]
    r_ak = [jnp.where(incl, g[n2:, n2:], 0.0).astype(BF16) for g in gram]
    inv = [eye + a for a in a_ab]
    pw = [a.astype(BF16) for a in a_ab]
    pw = [_dot(a, a) for a in pw]
    akv = [_dot(a_ak[p], v_s[p]).astype(BF16) for p in every]
    span = 2
    while 2 * span < c_len:
        pwb = [a.astype(BF16) for a in pw]
        both = [_dot(pwb[p], jnp.concatenate([pwb[p], inv[p].astype(BF16)], axis=1)) for p in every]
        pw = [b[:, :n2] for b in both]
        inv = [inv[p] + both[p][:, n2:] for p in every]
        span *= 2
    inv = [inv[p] + _dot(pw[p].astype(BF16), inv[p].astype(BF16)) for p in every]
    wx = [_dot(inv[p].astype(BF16), jnp.concatenate([a_t[p], akv[p]], axis=1)).astype(BF16)
          for p in every]
    qy = [_dot(r_ab[p], wx[p]) for p in every]
    rv = [_dot(r_ak[p], v_s[p]) for p in every]
    mn = [_dot(b_e[p], wx[p], TN) for p in every]
    kv = [_dot(k_e[p], v_s[p], TN) for p in every]
    for p in every:
        q_s = r_t[p] + qy[p][:, :n2]
        y_s = qy[p][:, n2:] + rv[p]
        m_o[0, p] = eye * jnp.exp(tot[p]) + mn[p][:, :n2]
        n_o[0, p] = mn[p][:, n2:] + kv[p]
        q_o[:, sl[p]] = q_s[:c_len] + q_s[c_len:]
        y_o[:, sl[p]] = y_s[:c_len] + y_s[c_len:]


def _rwkv_chunks(r, k, v, lw, an, bn):
    t_dim = r.shape[0]
    pairs = RWKV_PAIRS
    n_chunks = t_dim // RWKV_CHUNK
    tile = pl.BlockSpec((RWKV_CHUNK, pairs * LANES), lambda c, p: (c, p))
    op = pl.BlockSpec((1, pairs, LANES, LANES), lambda c, p: (c, p, 0, 0))
    op_shape = jax.ShapeDtypeStruct((n_chunks, RWKV_PAIRS, LANES, LANES), F32)
    row_shape = jax.ShapeDtypeStruct((t_dim, RWKV_WIDTH), F32)
    return pl.pallas_call(
        functools.partial(_rwkv_chunk_kernel, pairs=pairs),
        grid=(n_chunks, RWKV_PAIRS // pairs),
        in_specs=[tile] * 6,
        out_specs=[op, op, tile, tile],
        out_shape=[op_shape, op_shape, row_shape, row_shape],
        compiler_params=_cparams(("arbitrary", "arbitrary")),
        name="rwkv_chunk",
    )(r, k, v, lw, an, bn)


def _rwkv_sweep_kernel(m_ref, n_ref, q_ref, y0_ref, r_ref, k_ref, v_ref, g_ref,
                       rk_ref, lg_ref, lb_ref, bd_ref, o_ref, z_ref, y_ref, *, pairs, chunks):
    @pl.when(pl.program_id(1) == 0)
    def _():
        z_ref[...] = jnp.zeros_like(z_ref)

    sl = [slice(p * LANES, (p + 1) * LANES) for p in range(pairs)]
    z = [z_ref[p] for p in range(pairs)]
    for c in range(chunks):
        rows = slice(c * RWKV_CHUNK, (c + 1) * RWKV_CHUNK)
        zs = [_split(zp) for zp in z]
        ys = [_dot3s(q_ref[rows, sl[p]], zs[p]) for p in range(pairs)]
        zn = [_dot3s(m_ref[c, p], zs[p]) for p in range(pairs)]
        for p in range(pairs):
            y_ref[rows, sl[p]] = ys[p] + y0_ref[rows, sl[p]]
        z = [zn[p] + n_ref[c, p] for p in range(pairs)]
    for p in range(pairs):
        z_ref[p] = z[p]

    bd = bd_ref[...]
    y = y_ref[...]
    mu = _head_sum(y, bd) * (1.0 / RWKV_HEAD_DIM)
    yc = y - mu
    var = _head_sum(yc * yc, bd) * (1.0 / RWKV_HEAD_DIM)
    yn = yc * lax.rsqrt(var + RWKV_GN_EPS) * lg_ref[...] + lb_ref[...]
    bonus = _head_sum(r_ref[...] * k_ref[...] * rk_ref[...], bd) * v_ref[...]
    o_ref[...] = ((yn + bonus) * g_ref[...]).astype(o_ref.dtype)


def _rwkv_sweep(m_op, n_op, q_op, y0, r, k, v, g, r_k, ln_g, ln_b, bd, batch, seq):
    t_dim = q_op.shape[0]
    pairs = RWKV_PAIRS
    chunks = min(4, seq // RWKV_CHUNK)
    steps = seq // (RWKV_CHUNK * chunks)
    rows = RWKV_CHUNK * chunks
    tile = pl.BlockSpec((rows, RWKV_WIDTH), lambda b, s: (b * steps + s, 0))
    op = pl.BlockSpec((chunks, pairs, LANES, LANES), lambda b, s: (b * steps + s, 0, 0, 0))
    vec = pl.BlockSpec((1, RWKV_WIDTH), lambda b, s: (0, 0))
    return pl.pallas_call(
        functools.partial(_rwkv_sweep_kernel, pairs=pairs, chunks=chunks),
        grid=(batch, steps),
        in_specs=[op, op] + [tile] * 6 + [vec] * 3 + [pl.BlockSpec((LANES, LANES), lambda b, s: (0, 0))],
        out_specs=tile,
        out_shape=jax.ShapeDtypeStruct((t_dim, RWKV_WIDTH), BF16),
        scratch_shapes=[pltpu.VMEM((pairs, LANES, LANES), F32), pltpu.VMEM((rows, RWKV_WIDTH), F32)],
        compiler_params=_cparams(("arbitrary", "arbitrary"), VMEM_BUDGET),
        name="rwkv_sweep",
    )(m_op, n_op, q_op, y0, r, k, v, g, r_k, ln_g, ln_b, bd)


def _merge_kernel(yp_ref, ya_ref, yr_ref, wp_ref, wa_ref, wr_ref, g0_ref, g1_ref, g2_ref,
                  b0_ref, b1_ref, b2_ref, o_ref, wpb, wab, wrb):
    @pl.when(pl.program_id(1) == 0)
    def _():
        wpb[...] = wp_ref[...].astype(BF16)
        wab[...] = wa_ref[...].astype(BF16)
        wrb[...] = wr_ref[...].astype(BF16)

    m = jax.nn.sigmoid(g0_ref[...] + b0_ref[...]) * _dot(yp_ref[...], wpb[...])
    m = m + jax.nn.sigmoid(g1_ref[...] + b1_ref[...]) * _dot(ya_ref[...], wab[...])
    m = m + jax.nn.sigmoid(g2_ref[...] + b2_ref[...]) * _dot(yr_ref[...], wrb[...])
    o_ref[...] = m.astype(o_ref.dtype)


def _merge(y_pool, y_attn, y_rwkv, w_pool, w_attn, w_rwkv, layer, gate_logits, b_gate):
    t_dim = y_pool.shape[0]
    tm = min(1024, t_dim)
    tn = 512
    nb = D_MODEL // tn
    act = lambda width: pl.BlockSpec((tm, width), lambda j, i: (i, 0))
    wgt = lambda width: pl.BlockSpec((None, width, tn), lambda j, i: (layer, 0, j))
    gate = lambda br: pl.BlockSpec((tm, tn), lambda j, i: (i, br * nb + j))
    bias = lambda br: pl.BlockSpec((1, tn), lambda j, i: (0, br * nb + j))
    return pl.pallas_call(
        _merge_kernel, grid=(nb, t_dim // tm),
        in_specs=[act(POOL_WIDTH), act(SB_WIDTH), act(RWKV_WIDTH),
                  wgt(POOL_WIDTH), wgt(SB_WIDTH), wgt(RWKV_WIDTH),
                  gate(0), gate(1), gate(2), bias(0), bias(1), bias(2)],
        out_specs=pl.BlockSpec((tm, tn), lambda j, i: (i, j)),
        out_shape=jax.ShapeDtypeStruct((t_dim, D_MODEL), BF16),
        scratch_shapes=[pltpu.VMEM((POOL_WIDTH, tn), BF16), pltpu.VMEM((SB_WIDTH, tn), BF16),
                        pltpu.VMEM((RWKV_WIDTH, tn), BF16)],
        compiler_params=_cparams(("arbitrary", "arbitrary"), VMEM_BUDGET),
        name="merge",
    )(y_pool, y_attn, y_rwkv, w_pool, w_attn, w_rwkv, gate_logits, gate_logits, gate_logits,
      b_gate, b_gate, b_gate)


def _ffn_up_kernel(a_ref, wa_ref, wl_ref, cw_ref, cb_ref, o_ref, wb_ref, halo_ref, *, tiles_per_seq):
    i = pl.program_id(1)
    tm, tn = o_ref.shape

    @pl.when(i == 0)
    def _():
        _cast_panel(wa_ref, wb_ref.at[:, :tn])
        _cast_panel(wl_ref, wb_ref.at[:, tn:])
        halo_ref[...] = jnp.zeros_like(halo_ref)

    prev = jnp.where(i % tiles_per_seq == 0, 0.0, halo_ref[...])
    row = lax.broadcasted_iota(jnp.int32, (DOT_ROWS, 1), 0)
    for c in range(tm // DOT_ROWS):
        rows = slice(c * DOT_ROWS, (c + 1) * DOT_ROWS)
        res = _dot(a_ref[rows, :], wb_ref[...])
        act, lin = res[:, :tn], res[:, tn:]
        p1 = jnp.where(row == 0, prev[7:8, :], pltpu.roll(act, 1, axis=0))
        p2 = jnp.where(row == 0, prev[6:7, :],
                       jnp.where(row == 1, prev[7:8, :], pltpu.roll(act, 2, axis=0)))
        cv = p2 * cw_ref[0:1, :] + p1 * cw_ref[1:2, :] + act * cw_ref[2:3, :] + cb_ref[...]
        gelu = 0.5 * cv * (1.0 + lax.erf(cv * (2.0 ** -0.5)))
        o_ref[rows, :] = (gelu * lin).astype(o_ref.dtype)
        prev = act[DOT_ROWS - 8:, :]
    halo_ref[...] = prev


def _ffn_up(a, w_up, layer, conv_w, conv_b, seq):
    t_dim, k_dim = a.shape
    tm = min(2048, seq)
    tn = 256
    nb = D_FF // tn
    return pl.pallas_call(
        functools.partial(_ffn_up_kernel, tiles_per_seq=seq // tm),
        grid=(nb, t_dim // tm),
        in_specs=[pl.BlockSpec((tm, k_dim), lambda j, i: (i, 0)),
                  pl.BlockSpec((None, k_dim, tn), lambda j, i: (layer, 0, j)),
                  pl.BlockSpec((None, k_dim, tn), lambda j, i: (layer, 0, nb + j)),
                  pl.BlockSpec((3, tn), lambda j, i: (0, j)),
                  pl.BlockSpec((1, tn), lambda j, i: (0, j))],
        out_specs=pl.BlockSpec((tm, tn), lambda j, i: (i, j)),
        out_shape=jax.ShapeDtypeStruct((t_dim, D_FF), BF16),
        scratch_shapes=[pltpu.VMEM((k_dim, 2 * tn), BF16), pltpu.VMEM((8, tn), F32)],
        compiler_params=_cparams(("arbitrary", "arbitrary"), VMEM_BUDGET),
        name="ffn_up",
    )(a, w_up, w_up, conv_w, conv_b.reshape(1, D_FF))


def kernel(x, w_in, b_gate, pool_w, pool_scale, rwkv_mu, rwkv_w0, rwkv_w2, rwkv_a0, rwkv_a2, rwkv_g2, rwkv_k_k, rwkv_k_a, rwkv_r_k, rwkv_ln_g, rwkv_ln_b, w_branch_pool, w_branch_attn, w_branch_rwkv, w_out, ln1_g, ln1_b, w_up, ffn_conv_w, ffn_conv_b, w_down, ln2_g, ln2_b):
    batch, seq, _ = x.shape
    t_dim = batch * seq
    xf = x.reshape(t_dim, D_MODEL)
    xb = xf.astype(BF16)
    half = lax.broadcasted_iota(jnp.int32, (LANES, LANES), 0) // RWKV_HEAD_DIM
    bd = (half == half.T).astype(BF16)
    vec = lambda p: p.reshape(1, -1)
    col = lax.broadcasted_iota(jnp.int32, (1, N_MAIN), 1)
    q_cols = (col >= POOL_WIDTH) & (col < POOL_WIDTH + SB_WIDTH)
    main_scale = jnp.where(q_cols, SB_HEAD_DIM ** -0.5 * math.log2(math.e), 1.0).astype(F32)
    w_in_t = jnp.swapaxes(w_in, 1, 2)
    for l in range(DEPTH):
        main = _mm_t(xb, w_in_t, l, row0=0, n_cols=N_MAIN, out_dtype=BF16, tm=2048,
                     col_scale=main_scale)
        urw = _mm_t(xb, w_in_t, l, row0=N_MAIN, n_cols=RWKV_PAD, out_dtype=BF16, tm=2048)
        gate_logits = _mm_t(xb, w_in_t, l, row0=GATE_OFF, n_cols=3 * D_MODEL, out_dtype=BF16,
                            tm=2048)

        y_pool = _pool(main, pool_w[l], pool_scale[l], seq)
        y_attn = _attention(main, batch, seq)

        wc = jnp.zeros((RWKV_LOW, 3 * RWKV_WIDTH), F32)
        wc = wc.at[:RWKV_DECAY_RANK, :RWKV_WIDTH].set(rwkv_w2[l])
        wc = wc.at[RWKV_DECAY_RANK:RWKV_DECAY_RANK + RWKV_A_RANK, RWKV_WIDTH:2 * RWKV_WIDTH].set(rwkv_a2[l])
        lo = RWKV_DECAY_RANK + RWKV_A_RANK
        wc = wc.at[lo:lo + RWKV_GATE_RANK, 2 * RWKV_WIDTH:].set(rwkv_g2[l])
        mu = jnp.pad(rwkv_mu[l], (0, RWKV_PAD - RWKV_SHIFT_WIDTH)).reshape(1, RWKV_PAD)
        r, k, v, lw, an, bn, g = _rwkv_prep(urw, seq, mu, vec(rwkv_w0[l]), vec(rwkv_a0[l]),
                                            vec(rwkv_k_k[l]), vec(rwkv_k_a[l]), wc, bd)
        m_op, n_op, q_op, y0 = _rwkv_chunks(r, k, v, lw, an, bn)
        y_rwkv = _rwkv_sweep(m_op, n_op, q_op, y0, r, k, v, g, vec(rwkv_r_k[l]),
                             vec(rwkv_ln_g[l]), vec(rwkv_ln_b[l]), bd, batch, seq)

        merged = _merge(y_pool, y_attn, y_rwkv, w_branch_pool, w_branch_attn, w_branch_rwkv, l,
                        gate_logits, vec(b_gate[l]))
        sub = _mm(merged, w_out, l, out_dtype=BF16, tm=2048, tn=512)
        xf, xb = _ln(xf, sub, ln1_g[l], ln1_b[l], with_bf16=True)

        h = _ffn_up(xb, w_up, l, ffn_conv_w[l], ffn_conv_b[l], seq)
        ffn = _mm_prefetch(h, w_down, l, out_dtype=BF16, tm=512, tn=512)
        if l + 1 < DEPTH:
            xf, xb = _ln(xf, ffn, ln2_g[l], ln2_b[l], with_bf16=True)
        else:
            xf, = _ln(xf, ffn, ln2_g[l], ln2_b[l], with_bf16=False)
    return xf.reshape(batch, seq, D_MODEL)
```
